```python
import math
import jax, jax.numpy as jnp
from jax import lax
import numpy as np

D_MODEL = 4096
BATCH = 4
SEQ = 2048
DEPTH = 2
DEC_BATCH = 8
DEC_SEQ = 8
PAST_LEN = 16384
PAGE_SIZE = 128

HEAD_DIM = 128
CONV_CH = D_MODEL // 2
CONV_WIDTH = 31
SB_HEADS = (D_MODEL // 2) // HEAD_DIM
SB_W = SB_HEADS * HEAD_DIM
DIFF_HEADS = (D_MODEL // 2) // (2 * HEAD_DIM)
DC_W = DIFF_HEADS * 2 * HEAD_DIM
GMLP_CH = D_MODEL // 2
GMLP_GROUPS = GMLP_CH // HEAD_DIM
CHUNK = 128
QBLOCK = 128
D_FF = 4 * D_MODEL
N_BUCKETS = 32
MAX_DISTANCE = 128
LAMBDA_INIT = 0.8 - 0.6 * math.exp(-0.3 * 1)
EPS = 1e-6
D_IN_EVEN = 2 * CONV_CH + 3 * SB_W
D_IN_ODD = 3 * DC_W + 2 * GMLP_CH

kernel_name = 'hybrid_conformer_stickbreak_diffattn_gmlp_step'


def _rmsnorm(x, g):
    xf = x.astype(jnp.float32)
    y = xf * lax.rsqrt(jnp.mean(xf * xf, axis=-1, keepdims=True) + EPS)
    return (y * g).astype(x.dtype)


def _layernorm(x, g, b):
    xf = x.astype(jnp.float32)
    mu = jnp.mean(xf, axis=-1, keepdims=True)
    xc = xf - mu
    y = xc * lax.rsqrt(jnp.mean(xc * xc, axis=-1, keepdims=True) + EPS)
    return (y * g + b).astype(x.dtype)


def _qblock(t):
    return QBLOCK if t % QBLOCK == 0 else t


def _gather_pages(cache, page_table):
    g = cache[page_table]
    return g.reshape((g.shape[0], g.shape[1] * g.shape[2]) + cache.shape[2:])


def _t5_bucket(rel):
    n = jnp.maximum(rel, 0)
    max_exact = N_BUCKETS // 2
    nf = jnp.maximum(n, 1).astype(jnp.float32)
    large = max_exact + (jnp.log(nf / max_exact) / math.log(MAX_DISTANCE / max_exact)
                         * (N_BUCKETS - max_exact)).astype(jnp.int32)
    large = jnp.minimum(large, N_BUCKETS - 1)
    return jnp.where(n < max_exact, n, large)


def _stick_breaking(q, k, v, q_pos, k_pos):
    B, Tq, H, d = q.shape
    qb_len = _qblock(Tq)
    nb = Tq // qb_len
    qb = q.reshape(B, nb, qb_len, H, d).swapaxes(0, 1)
    pb = q_pos.reshape(nb, qb_len)
    scale = d ** -0.5

    def block(args):
        qi, pi = args
        z = jnp.einsum('bqhd,bkhd->bhqk', qi, k).astype(jnp.float32) * scale
        mask = k_pos[None, :] < pi[:, None]
        log_not = jnp.where(mask, jax.nn.log_sigmoid(-z), 0.0)
        log_rest = lax.cumsum(log_not, axis=3, reverse=True) - log_not
        a = jnp.where(mask, jnp.exp(jax.nn.log_sigmoid(z) + log_rest), 0.0)
        return jnp.einsum('bhqk,bkhd->bqhd', a.astype(v.dtype), v)

    out = lax.map(block, (qb, pb))
    return out.swapaxes(0, 1).reshape(B, Tq, H, d)


def _diff_attention(q, k, v, q_pos, k_pos, rel_bias, lam):
    B, Tq, H, _, d = q.shape
    qb_len = _qblock(Tq)
    nb = Tq // qb_len
    qb = q.reshape(B, nb, qb_len, H, 2, d).swapaxes(0, 1)
    pb = q_pos.reshape(nb, qb_len)
    scale = d ** -0.5
    neg = jnp.finfo(jnp.float32).min

    def block(args):
        qi, pi = args
        s = jnp.einsum('bqhcd,bkhcd->bchqk', qi, k).astype(jnp.float32) * scale
        rel = pi[:, None] - k_pos[None, :]
        bias = jnp.transpose(rel_bias[_t5_bucket(rel)], (2, 0, 1)).astype(jnp.float32)
        s = jnp.where(rel >= 0, s + bias, neg)
        p = jax.nn.softmax(s, axis=-1)
        a = p[:, 0] - lam * p[:, 1]
        return jnp.einsum('bhqk,bkhe->bqhe', a.astype(v.dtype), v)

    out = lax.map(block, (qb, pb))
    return out.swapaxes(0, 1).reshape(B, Tq, H, 2 * d)


def _even_mixer(h, pos, conv_state, past_sb, P):
    B, T, _ = h.shape
    proj = h @ P['w_in_even']
    a_val, a_gate, q, k, v = jnp.split(
        proj, [CONV_CH, 2 * CONV_CH, 2 * CONV_CH + SB_W, 2 * CONV_CH + 2 * SB_W], axis=-1)
    glu = a_val * jax.nn.sigmoid(a_gate)
    if conv_state is None:
        prefix = jnp.zeros((B, CONV_WIDTH - 1, CONV_CH), glu.dtype)
    else:
        prefix = conv_state.astype(glu.dtype)
    xx = jnp.concatenate([prefix, glu], axis=1)
    conv = lax.conv_general_dilated(
        xx, P['conv_w'][:, None, :].astype(xx.dtype), window_strides=(1,), padding='VALID',
        dimension_numbers=('NWC', 'WIO', 'NWC'), feature_group_count=CONV_CH) + P['conv_b']
    a_out = jax.nn.silu(_layernorm(conv, P['conv_ln_g'], P['conv_ln_b']))
    new_conv_state = xx[:, -(CONV_WIDTH - 1):]
    q = q.reshape(B, T, SB_HEADS, HEAD_DIM)
    k = k.reshape(B, T, SB_HEADS, HEAD_DIM)
    v = v.reshape(B, T, SB_HEADS, HEAD_DIM)
    if past_sb is None:
        keys, vals, k_pos = k, v, pos
    else:
        pk, pv = past_sb
        keys = jnp.concatenate([pk.astype(k.dtype), k], axis=1)
        vals = jnp.concatenate([pv.astype(v.dtype), v], axis=1)
        k_pos = jnp.concatenate([jnp.arange(pk.shape[1], dtype=jnp.int32), pos])
    sb = _stick_breaking(q, keys, vals, pos, k_pos).reshape(B, T, SB_W)
    y = jnp.concatenate([a_out, sb], axis=-1) @ P['w_out_even']
    return y, new_conv_state, k, v


def _odd_mixer(h, pos, past_dc, P):
    B, T, _ = h.shape
    proj = h @ P['w_in_odd']
    q, k, v, z = jnp.split(proj, [DC_W, 2 * DC_W, 3 * DC_W], axis=-1)
    q = _rmsnorm(q.reshape(B, T, DIFF_HEADS, 2, HEAD_DIM), P['q_norm_g'])
    k = _rmsnorm(k.reshape(B, T, DIFF_HEADS, 2, HEAD_DIM), P['k_norm_g'])
    v = v.reshape(B, T, DIFF_HEADS, 2 * HEAD_DIM)
    lam = (jnp.exp(jnp.sum(P['lambda_q1'].astype(jnp.float32) * P['lambda_k1'].astype(jnp.float32)))
           - jnp.exp(jnp.sum(P['lambda_q2'].astype(jnp.float32) * P['lambda_k2'].astype(jnp.float32)))
           + LAMBDA_INIT)
    if past_dc is None:
        keys, vals, k_pos = k, v, pos
    else:
        pk, pv = past_dc
        keys = jnp.concatenate([pk.astype(k.dtype), k], axis=1)
        vals = jnp.concatenate([pv.astype(v.dtype), v], axis=1)
        k_pos = jnp.concatenate([jnp.arange(pk.shape[1], dtype=jnp.int32), pos])
    attn = _diff_attention(q, keys, vals, pos, k_pos, P['rel_bias'], lam)
    attn = _rmsnorm(attn, P['sub_norm_g']) * (1.0 - LAMBDA_INIT)
    c_out = attn.reshape(B, T, DC_W)
    z = jax.nn.gelu(z)
    u, vv = jnp.split(z, 2, axis=-1)
    vv = _layernorm(vv, P['gmlp_ln_g'], P['gmlp_ln_b'])
    L = CHUNK if T % CHUNK == 0 else T
    nc = T // L
    w = jnp.tril(P['gmlp_ws'][:, :L, :L])
    vr = vv.reshape(B, nc, L, GMLP_GROUPS, HEAD_DIM)
    mixed = jnp.einsum('gts,bcsgd->bctgd', w, vr) + P['gmlp_bs'][:, :L].T[None, None, :, :, None]
    d_out = u * mixed.reshape(B, T, GMLP_CH)
    y = jnp.concatenate([c_out, d_out], axis=-1) @ P['w_out_odd']
    return y, k, v, vv[:, -L:]


def _sq_relu_mlp(h, w1, w2):
    a = jax.nn.relu(h @ w1)
    return (a * a) @ w2


def _run_group(x, c, pos, state_conv, past_sb, past_dc, P):
    conv_st = sb_k = sb_v = dc_k = dc_v = g_v = None
    for layer in range(DEPTH):
        mod = (jax.nn.silu(c) @ P['w_ada'][layer] + P['b_ada'][layer])[:, None, :]
        sh1, sc1, gt1, sh2, sc2, gt2 = jnp.split(mod, 6, axis=-1)
        h = _rmsnorm(x, P['norm_g'][layer, 0]) * (1.0 + sc1) + sh1
        if layer % 2 == 0:
            y, conv_st, sb_k, sb_v = _even_mixer(h, pos, state_conv, past_sb, P)
        else:
            y, dc_k, dc_v, g_v = _odd_mixer(h, pos, past_dc, P)
        x = x + gt1 * y
        h = _rmsnorm(x, P['norm_g'][layer, 1]) * (1.0 + sc2) + sh2
        x = x + gt2 * _sq_relu_mlp(h, P['w_mlp_in'][layer], P['w_mlp_out'][layer])
    return x, conv_st, sb_k, sb_v, dc_k, dc_v, g_v


def setup_inputs(seed: int = 0) -> dict:
    key = jax.random.key(seed)
    ks = jax.random.split(key, 40)
    f32 = jnp.float32
    n_pages = PAST_LEN // PAGE_SIZE
    n_used = DEC_BATCH * n_pages
    n_pool = max((n_used * 5) // 4, n_used + 1)
    nrm = lambda k, shape, s=1.0: (jax.random.normal(k, shape, f32) * s)
    page_table = jax.random.permutation(ks[0], n_pool)[:n_used].reshape(DEC_BATCH, n_pages).astype(jnp.int32)
    return {
        'x_prompt': nrm(ks[1], (BATCH, SEQ, D_MODEL)),
        'x_sample': nrm(ks[2], (DEC_BATCH, DEC_SEQ, D_MODEL)),
        'state_conv': nrm(ks[3], (DEC_BATCH, CONV_WIDTH - 1, CONV_CH), 0.5),
        'cache_sb_k': nrm(ks[4], (n_pool, PAGE_SIZE, SB_HEADS, HEAD_DIM)),
        'cache_sb_v': nrm(ks[5], (n_pool, PAGE_SIZE, SB_HEADS, HEAD_DIM)),
        'cache_dc_k': nrm(ks[6], (n_pool, PAGE_SIZE, DIFF_HEADS, 2, HEAD_DIM)),
        'cache_dc_v': nrm(ks[7], (n_pool, PAGE_SIZE, DIFF_HEADS, 2 * HEAD_DIM)),
        'page_table': page_table,
        'c_prompt': nrm(ks[8], (BATCH, D_MODEL)),
        'c_sample': nrm(ks[9], (DEC_BATCH, D_MODEL)),
        'norm_g': 1.0 + nrm(ks[10], (DEPTH, 2, D_MODEL), 0.05),
        'w_ada': nrm(ks[11], (DEPTH, D_MODEL, 6 * D_MODEL), 0.5 * D_MODEL ** -0.5),
        'b_ada': nrm(ks[12], (DEPTH, 6 * D_MODEL), 0.01),
        'w_in_even': nrm(ks[13], (D_MODEL, D_IN_EVEN), D_MODEL ** -0.5),
        'w_out_even': nrm(ks[14], (CONV_CH + SB_W, D_MODEL), (CONV_CH + SB_W) ** -0.5),
        'conv_w': nrm(ks[15], (CONV_WIDTH, CONV_CH), CONV_WIDTH ** -0.5),
        'conv_b': nrm(ks[16], (CONV_CH,), 0.01),
        'conv_ln_g': 1.0 + nrm(ks[17], (CONV_CH,), 0.05),
        'conv_ln_b': nrm(ks[18], (CONV_CH,), 0.01),
        'w_in_odd': nrm(ks[19], (D_MODEL, D_IN_ODD), D_MODEL ** -0.5),
        'w_out_odd': nrm(ks[20], (DC_W + GMLP_CH, D_MODEL), (DC_W + GMLP_CH) ** -0.5),
        'q_norm_g': 1.0 + nrm(ks[21], (HEAD_DIM,), 0.05),
        'k_norm_g': 1.0 + nrm(ks[22], (HEAD_DIM,), 0.05),
        'lambda_q1': nrm(ks[23], (HEAD_DIM,), 0.1),
        'lambda_k1': nrm(ks[24], (HEAD_DIM,), 0.1),
        'lambda_q2': nrm(ks[25], (HEAD_DIM,), 0.1),
        'lambda_k2': nrm(ks[26], (HEAD_DIM,), 0.1),
        'sub_norm_g': 1.0 + nrm(ks[27], (2 * HEAD_DIM,), 0.05),
        'rel_bias': nrm(ks[28], (N_BUCKETS, DIFF_HEADS), 0.5),
        'gmlp_ln_g': 1.0 + nrm(ks[29], (GMLP_CH,), 0.05),
        'gmlp_ln_b': nrm(ks[30], (GMLP_CH,), 0.01),
        'gmlp_ws': nrm(ks[31], (GMLP_GROUPS, CHUNK, CHUNK), CHUNK ** -0.5),
        'gmlp_bs': 1.0 + nrm(ks[32], (GMLP_GROUPS, CHUNK), 0.05),
        'w_mlp_in': nrm(ks[33], (DEPTH, D_MODEL, D_FF), D_MODEL ** -0.5),
        'w_mlp_out': nrm(ks[34], (DEPTH, D_FF, D_MODEL), D_FF ** -0.5),
    }


def reference(x_prompt, x_sample, state_conv, cache_sb_k, cache_sb_v, cache_dc_k, cache_dc_v,
              page_table, c_prompt, c_sample, norm_g, w_ada, b_ada, w_in_even, w_out_even,
              conv_w, conv_b, conv_ln_g, conv_ln_b, w_in_odd, w_out_odd, q_norm_g, k_norm_g,
              lambda_q1, lambda_k1, lambda_q2, lambda_k2, sub_norm_g, rel_bias,
              gmlp_ln_g, gmlp_ln_b, gmlp_ws, gmlp_bs, w_mlp_in, w_mlp_out):
    P = dict(norm_g=norm_g, w_ada=w_ada, b_ada=b_ada, w_in_even=w_in_even, w_out_even=w_out_even,
             conv_w=conv_w, conv_b=conv_b, conv_ln_g=conv_ln_g, conv_ln_b=conv_ln_b,
             w_in_odd=w_in_odd, w_out_odd=w_out_odd, q_norm_g=q_norm_g, k_norm_g=k_norm_g,
             lambda_q1=lambda_q1, lambda_k1=lambda_k1, lambda_q2=lambda_q2, lambda_k2=lambda_k2,
             sub_norm_g=sub_norm_g, rel_bias=rel_bias, gmlp_ln_g=gmlp_ln_g, gmlp_ln_b=gmlp_ln_b,
             gmlp_ws=gmlp_ws, gmlp_bs=gmlp_bs, w_mlp_in=w_mlp_in, w_mlp_out=w_mlp_out)
    pos_p = jnp.arange(x_prompt.shape[1], dtype=jnp.int32)
    (y_prompt, conv_state_prompt, sb_k_prompt, sb_v_prompt,
     dc_k_prompt, dc_v_prompt, gmlp_v_prompt) = _run_group(x_prompt, c_prompt, pos_p, None, None, None, P)
    pos_s = PAST_LEN + jnp.arange(x_sample.shape[1], dtype=jnp.int32)
    past_sb = (_gather_pages(cache_sb_k, page_table), _gather_pages(cache_sb_v, page_table))
    past_dc = (_gather_pages(cache_dc_k, page_table), _gather_pages(cache_dc_v, page_table))
    (y_sample, conv_state_sample, sb_k_sample, sb_v_sample,
     dc_k_sample, dc_v_sample, gmlp_v_sample) = _run_group(x_sample, c_sample, pos_s, state_conv, past_sb, past_dc, P)
    return (y_prompt, y_sample, conv_state_prompt, conv_state_sample,
            sb_k_prompt, sb_v_prompt, sb_k_sample, sb_v_sample,
            dc_k_prompt, dc_v_prompt, dc_k_sample, dc_v_sample,
            gmlp_v_prompt, gmlp_v_sample)
```

```python
import functools
import math

import jax
import jax.numpy as jnp
from jax import lax
from jax.experimental import pallas as pl
from jax.experimental.pallas import tpu as pltpu

F32 = jnp.float32
BF16 = jnp.bfloat16

EPS = 1e-6
LAMBDA_INIT = 0.8 - 0.6 * math.exp(-0.3 * 1)
MAX_DISTANCE = 128
CONV_PAD = 32
LANES = 128
NEG = -1e30

VMEM_LIMIT = 56 * 1024 * 1024


def _cparams(sem):
    return pltpu.CompilerParams(dimension_semantics=sem, vmem_limit_bytes=VMEM_LIMIT)


def _tile(dim, pref, align):
    if dim <= pref:
        return dim
    t = (pref // align) * align
    while t >= align:
        if dim % t == 0:
            return t
        t -= align
    return dim


def _norm_mod_kernel(x_ref, g_ref, sc_ref, sh_ref, o_ref):
    x = x_ref[0]
    ms = jnp.mean(x * x, axis=-1, keepdims=True)
    y = x * lax.rsqrt(ms + EPS) * g_ref[...]
    o_ref[0] = (y * (1.0 + sc_ref[0]) + sh_ref[0]).astype(o_ref.dtype)


def _norm_mod(x, g, sc, sh, out_dtype):
    B, T, D = x.shape
    tr = _tile(T, 256, 16)
    return pl.pallas_call(
        _norm_mod_kernel,
        grid=(B, T // tr),
        in_specs=[
            pl.BlockSpec((1, tr, D), lambda b, t: (b, t, 0)),
            pl.BlockSpec((1, D), lambda b, t: (0, 0)),
            pl.BlockSpec((1, 1, D), lambda b, t: (b, 0, 0)),
            pl.BlockSpec((1, 1, D), lambda b, t: (b, 0, 0)),
        ],
        out_specs=pl.BlockSpec((1, tr, D), lambda b, t: (b, t, 0)),
        out_shape=jax.ShapeDtypeStruct((B, T, D), out_dtype),
        compiler_params=_cparams(("parallel", "parallel")),
        name="norm_mod",
    )(x, g.reshape(1, D), sc.reshape(B, 1, D), sh.reshape(B, 1, D))


def _gelu_tanh(x):
    c = math.sqrt(2.0 / math.pi)
    return x * (0.5 * (1.0 + jnp.tanh(c * (x + 0.044715 * (x * x * x)))))


def _mm_kernel(*refs, nk, epilogue, a_silu, has_vec, has_res):
    it = iter(refs)
    a_ref = next(it)
    w_ref = next(it)
    vec_ref = next(it) if has_vec else None
    res_ref = next(it) if has_res else None
    gate_ref = next(it) if has_res else None
    o_ref = next(it)
    acc_ref = next(it) if nk > 1 else None

    a = a_ref[...]
    if a_silu:
        a = a * jax.nn.sigmoid(a)
    a = a.astype(BF16)
    w = w_ref[...].astype(BF16)
    part = jnp.dot(a, w, preferred_element_type=F32)

    def finish(acc):
        if epilogue == "relu2":
            r = jnp.maximum(acc, 0.0)
            out = r * r
        elif epilogue == "gelu":
            out = _gelu_tanh(acc)
        elif epilogue == "bias":
            out = acc + vec_ref[...]
        elif epilogue == "group_rmsnorm":
            tn = acc.shape[1]
            for g in range(tn // LANES):
                sl = slice(g * LANES, (g + 1) * LANES)
                x = acc[:, sl]
                ms = jnp.mean(x * x, axis=-1, keepdims=True)
                o_ref[:, sl] = (x * lax.rsqrt(ms + EPS) * vec_ref[:, sl]).astype(o_ref.dtype)
            return
        else:
            out = acc
        if has_res:
            gate = gate_ref[0] if len(gate_ref.shape) == 3 else gate_ref[...]
            out = res_ref[...] + gate * out
        o_ref[...] = out.astype(o_ref.dtype)

    if nk == 1:
        finish(part)
    else:
        k = pl.program_id(2)

        @pl.when(k == 0)
        def _():
            acc_ref[...] = part

        @pl.when(k > 0)
        def _():
            acc_ref[...] += part

        @pl.when(k == nk - 1)
        def _():
            finish(acc_ref[...])


def _matmul(a, w, *, layer=0, col_off=0, n_out=None, epilogue="none", a_silu=False,
            vec=None, res=None, gate=None, rows_per_gate=None, out_dtype=F32, tn_pref=1024):
    M, K = a.shape
    n_out = w.shape[2] - col_off if n_out is None else n_out
    tm = _tile(M if rows_per_gate is None else rows_per_gate, 1024, 16)
    tn = _tile(math.gcd(n_out, col_off), tn_pref, LANES)
    tk = _tile(K, 4096, LANES)
    assert col_off % tn == 0
    nk = K // tk
    joff = col_off // tn
    grid = (M // tm, n_out // tn, nk)

    in_specs = [
        pl.BlockSpec((tm, tk), lambda i, j, k: (i, k)),
        pl.BlockSpec((None, tk, tn), lambda i, j, k: (layer, k, j + joff)),
    ]
    args = [a, w]
    if vec is not None:
        in_specs.append(pl.BlockSpec((1, tn), lambda i, j, k: (0, j)))
        args.append(vec.reshape(1, n_out))
    if res is not None:
        in_specs.append(pl.BlockSpec((tm, tn), lambda i, j, k: (i, j)))
        args.append(res)
        if gate.ndim == 3:
            assert rows_per_gate % tm == 0
            rpg = rows_per_gate // tm
            in_specs.append(pl.BlockSpec((1, 1, tn), lambda i, j, k: (i // rpg, 0, j)))
        else:
            in_specs.append(pl.BlockSpec((tm, tn), lambda i, j, k: (i, j)))
        args.append(gate)
    scratch = [pltpu.VMEM((tm, tn), F32)] if nk > 1 else []
    kern = functools.partial(_mm_kernel, nk=nk, epilogue=epilogue, a_silu=a_silu,
                             has_vec=vec is not None, has_res=res is not None)
    return pl.pallas_call(
        kern,
        grid=grid,
        in_specs=in_specs,
        out_specs=pl.BlockSpec((tm, tn), lambda i, j, k: (i, j)),
        out_shape=jax.ShapeDtypeStruct((M, n_out), out_dtype),
        scratch_shapes=scratch,
        compiler_params=_cparams(("parallel", "parallel", "arbitrary")),
        name="mm_" + epilogue,
    )(*args)


def _conv_kernel(val_ref, gate_ref, pre_ref, w_ref, b_ref, lg_ref, lb_ref, o_ref, st_ref,
                 xx_ref, cv_ref, *, tr, width, rc, cc):
    t = pl.program_id(1)
    nt = pl.num_programs(1)
    C = val_ref.shape[1]

    @pl.when(t == 0)
    def _():
        xx_ref[0:CONV_PAD, :] = pre_ref[0]

    xx_ref[CONV_PAD:CONV_PAD + tr, :] = val_ref[...] * jax.nn.sigmoid(gate_ref[...])

    base = CONV_PAD - (width - 1)
    for r0 in range(0, tr, rc):
        for c0 in range(0, C, cc):
            acc = jnp.zeros((rc, cc), F32)
            for k in range(width):
                acc = acc + w_ref[k:k + 1, c0:c0 + cc] * xx_ref[base + r0 + k:base + r0 + k + rc, c0:c0 + cc]
            cv_ref[r0:r0 + rc, c0:c0 + cc] = acc + b_ref[:, c0:c0 + cc]

    cv = cv_ref[...]
    mu = jnp.mean(cv, axis=-1, keepdims=True)
    xc = cv - mu
    var = jnp.mean(xc * xc, axis=-1, keepdims=True)
    y = xc * lax.rsqrt(var + EPS) * lg_ref[...] + lb_ref[...]
    o_ref[...] = (y * jax.nn.sigmoid(y)).astype(o_ref.dtype)

    @pl.when(t == nt - 1)
    def _():
        st_ref[0] = xx_ref[tr:tr + CONV_PAD, :]

    @pl.when(t < nt - 1)
    def _():
        xx_ref[0:CONV_PAD, :] = xx_ref[tr:tr + CONV_PAD, :]


def _conv_module(proj, B, T, C, prefix, conv_w, conv_b, ln_g, ln_b, out_dtype):
    width = conv_w.shape[0]
    hist = width - 1
    assert hist <= CONV_PAD
    tr = _tile(T, 256, 8)
    nt = T // tr
    assert nt == 1 or tr >= CONV_PAD
    if prefix is None:
        pre = jnp.zeros((B, CONV_PAD, C), F32)
    else:
        pre = jnp.pad(prefix.astype(F32), ((0, 0), (CONV_PAD - hist, 0), (0, 0)))
    rc = _tile(tr, 32, 8)
    cc = _tile(C, 512, LANES)
    kern = functools.partial(_conv_kernel, tr=tr, width=width, rc=rc, cc=cc)
    row = lambda b, t: (b * nt + t, 0)
    a_out, st = pl.pallas_call(
        kern,
        grid=(B, nt),
        in_specs=[
            pl.BlockSpec((tr, C), row),
            pl.BlockSpec((tr, C), lambda b, t: (b * nt + t, 1)),
            pl.BlockSpec((1, CONV_PAD, C), lambda b, t: (b, 0, 0)),
            pl.BlockSpec((width, C), lambda b, t: (0, 0)),
            pl.BlockSpec((1, C), lambda b, t: (0, 0)),
            pl.BlockSpec((1, C), lambda b, t: (0, 0)),
            pl.BlockSpec((1, C), lambda b, t: (0, 0)),
        ],
        out_specs=[
            pl.BlockSpec((tr, C), row),
            pl.BlockSpec((1, CONV_PAD, C), lambda b, t: (b, 0, 0)),
        ],
        out_shape=[
            jax.ShapeDtypeStruct((B * T, C), out_dtype),
            jax.ShapeDtypeStruct((B, CONV_PAD, C), F32),
        ],
        scratch_shapes=[pltpu.VMEM((CONV_PAD + tr, C), F32), pltpu.VMEM((tr, C), F32)],
        compiler_params=_cparams(("parallel", "arbitrary")),
        name="conv_module",
    )(proj, proj, pre, conv_w, conv_b.reshape(1, C), ln_g.reshape(1, C), ln_b.reshape(1, C))
    return a_out, st[:, CONV_PAD - hist:, :]


def _dot_nt(a, b):
    return lax.dot_general(a, b, (((1,), (1,)), ((), ())), preferred_element_type=F32)


def _sb_block(z, mask, c, tri):
    sp = jnp.maximum(z, 0.0) + jnp.log1p(jnp.exp(-jnp.abs(z)))
    ln = -sp if mask is None else jnp.where(mask, -sp, 0.0)
    hi = ln.astype(BF16)
    lo = (ln - hi.astype(F32)).astype(BF16)
    within = (jnp.dot(hi, tri, preferred_element_type=F32)
              + jnp.dot(lo, tri, preferred_element_type=F32))
    a = jnp.exp(z - sp + within + c)
    if mask is not None:
        a = jnp.where(mask, a, 0.0)
    return a, c + jnp.sum(ln, axis=-1, keepdims=True)


def _strict_lower(n):
    r = lax.broadcasted_iota(jnp.int32, (n, n), 0)
    c = lax.broadcasted_iota(jnp.int32, (n, n), 1)
    return r, c


def _sb_prompt_kernel(q_ref, k_ref, v_ref, o_ref, *, scale, blk):
    i = pl.program_id(2)
    q = q_ref[...].astype(BF16)
    r, c_io = _strict_lower(blk)
    tri = jnp.where(r > c_io, 1.0, 0.0).astype(BF16)
    causal = c_io < r

    def body(t, carry):
        c, acc = carry
        off = pl.multiple_of((i - t) * blk, blk)
        kj = k_ref[pl.ds(off, blk), :].astype(BF16)
        vj = v_ref[pl.ds(off, blk), :].astype(BF16)
        z = _dot_nt(q, kj) * scale
        mask = jnp.logical_or(causal, t > 0)
        a, c = _sb_block(z, mask, c, tri)
        acc = acc + jnp.dot(a.astype(BF16), vj, preferred_element_type=F32)
        return c, acc

    d = v_ref.shape[1]
    _, acc = lax.fori_loop(0, i + 1, body, (jnp.zeros((blk, 1), F32), jnp.zeros((blk, d), F32)))
    o_ref[...] = acc.astype(o_ref.dtype)


def _sb_prompt(q, k, v, B, T, H, q_off, out_dtype):
    blk = LANES if T % LANES == 0 else T
    nq = T // blk
    d = LANES
    kern = functools.partial(_sb_prompt_kernel, scale=d ** -0.5, blk=blk)
    return pl.pallas_call(
        kern,
        grid=(B, H, nq),
        in_specs=[
            pl.BlockSpec((blk, d), lambda b, h, i: (b * nq + i, q_off + h)),
            pl.BlockSpec((T, d), lambda b, h, i: (b, h)),
            pl.BlockSpec((T, d), lambda b, h, i: (b, h)),
        ],
        out_specs=pl.BlockSpec((blk, d), lambda b, h, i: (b * nq + i, h)),
        out_shape=jax.ShapeDtypeStruct((B * T, H * d), out_dtype),
        compiler_params=_cparams(("parallel", "parallel", "arbitrary")),
        name="sb_prompt",
    )(q, k, v)


def _block_diag_mask(rows, cols, row_shift, col_shift):
    r = lax.broadcasted_iota(jnp.int32, (rows, cols), 0)
    c = lax.broadcasted_iota(jnp.int32, (rows, cols), 1)
    return jnp.right_shift(r, row_shift) == jnp.right_shift(c, col_shift)


def _sum_row_groups(x, group):
    out = x[0:group]
    for g in range(1, x.shape[0] // group):
        out = out + x[g * group:(g + 1) * group]
    return out


def _sb_sample_kernel(pt_ref, q_ref, kn_ref, vn_ref, kc_ref, vc_ref, o_ref,
                      qbd_ref, c_ref, acc_ref, *, scale, dec, groups):
    s = pl.program_id(1)
    ns = pl.num_programs(1)
    R = dec * groups
    dshift = dec.bit_length() - 1
    lshift = LANES.bit_length() - 1
    r, c_io = _strict_lower(LANES)
    tri = jnp.where(r > c_io, 1.0, 0.0).astype(BF16)

    @pl.when(s == 0)
    def _():
        q = q_ref[0]
        qrep = jnp.concatenate([q] * groups, axis=0)
        bd = _block_diag_mask(R, q.shape[1], dshift, lshift)
        qbd_ref[...] = jnp.where(bd, qrep, 0.0).astype(BF16)
        c_ref[...] = jnp.zeros_like(c_ref)
        acc_ref[...] = jnp.zeros_like(acc_ref)

    def process(k, v, mask):
        z = _dot_nt(qbd_ref[...], k.astype(BF16)) * scale
        a, c = _sb_block(z, mask, c_ref[...], tri)
        c_ref[...] = c
        acc_ref[...] += jnp.dot(a.astype(BF16), v.astype(BF16), preferred_element_type=F32)

    @pl.when(s == 0)
    def _():
        rr = lax.broadcasted_iota(jnp.int32, (R, LANES), 0)
        cc = lax.broadcasted_iota(jnp.int32, (R, LANES), 1)
        process(kn_ref[0], vn_ref[0], cc < jnp.bitwise_and(rr, dec - 1))

    @pl.when(s > 0)
    def _():
        process(kc_ref[0], vc_ref[0], None)

    @pl.when(s == ns - 1)
    def _():
        bd = _block_diag_mask(R, acc_ref.shape[1], dshift, lshift)
        o_ref[0] = _sum_row_groups(jnp.where(bd, acc_ref[...], 0.0), dec)


def _pad_rows(x, rows):
    return jnp.pad(x, ((0, 0), (0, rows - x.shape[1]), (0, 0)))


def _sb_sample(q, k_new, v_new, cache_k, cache_v, page_table, Bs, Ts):
    W = q.shape[2]
    groups = W // LANES
    n_pages = page_table.shape[1]
    page = cache_k.shape[1]
    assert page == LANES and Ts & (Ts - 1) == 0
    kern = functools.partial(_sb_sample_kernel, scale=LANES ** -0.5, dec=Ts, groups=groups)
    R = Ts * groups
    page_idx = lambda b, s, pt: (pt[b, n_pages - jnp.maximum(s, 1)], 0, 0)
    grid_spec = pltpu.PrefetchScalarGridSpec(
        num_scalar_prefetch=1,
        grid=(Bs, n_pages + 1),
        in_specs=[
            pl.BlockSpec((1, Ts, W), lambda b, s, pt: (b, 0, 0)),
            pl.BlockSpec((1, page, W), lambda b, s, pt: (b, 0, 0)),
            pl.BlockSpec((1, page, W), lambda b, s, pt: (b, 0, 0)),
            pl.BlockSpec((1, page, W), page_idx),
            pl.BlockSpec((1, page, W), page_idx),
        ],
        out_specs=pl.BlockSpec((1, Ts, W), lambda b, s, pt: (b, 0, 0)),
        scratch_shapes=[pltpu.VMEM((R, W), BF16), pltpu.VMEM((R, 1), F32), pltpu.VMEM((R, W), F32)],
    )
    return pl.pallas_call(
        kern,
        grid_spec=grid_spec,
        out_shape=jax.ShapeDtypeStruct((Bs, Ts, W), F32),
        compiler_params=_cparams(("parallel", "arbitrary")),
        name="sb_sample",
    )(page_table, q, _pad_rows(k_new, page), _pad_rows(v_new, page), cache_k, cache_v)


def _t5_bucket(rel, n_buckets):
    n = jnp.maximum(rel, 0)
    max_exact = n_buckets // 2
    nf = jnp.maximum(n, 1).astype(F32)
    large = max_exact + (jnp.log(nf / max_exact) / math.log(MAX_DISTANCE / max_exact)
                         * (n_buckets - max_exact)).astype(jnp.int32)
    large = jnp.minimum(large, n_buckets - 1)
    return jnp.where(n < max_exact, n, large)


def _lambda(lv_ref):
    s1 = jnp.sum(lv_ref[0:1, :] * lv_ref[1:2, :], axis=-1, keepdims=True)
    s2 = jnp.sum(lv_ref[2:3, :] * lv_ref[3:4, :], axis=-1, keepdims=True)
    return jnp.exp(s1) - jnp.exp(s2) + LAMBDA_INIT


def _softmax_step(s, m, l):
    m_new = jnp.maximum(m, jnp.max(s, axis=-1, keepdims=True))
    alpha = jnp.exp(m - m_new)
    p = jnp.exp(s - m_new)
    return p, alpha, m_new, alpha * l + jnp.sum(p, axis=-1, keepdims=True)


def _dc_prompt_kernel(q_ref, k_ref, v_ref, tab_ref, far_ref, lv_ref, g_ref, o_ref,
                      m_ref, l_ref, acc_ref, *, scale, blk):
    i = pl.program_id(2)
    d = LANES
    q = [q_ref[:, c * d:(c + 1) * d].astype(BF16) for c in range(2)]
    m_ref[...] = jnp.full_like(m_ref, NEG)
    l_ref[...] = jnp.zeros_like(l_ref)
    acc_ref[...] = jnp.zeros_like(acc_ref)

    def step(j, bias, mask):
        off = pl.multiple_of(j * blk, blk)
        vj = v_ref[pl.ds(off, blk), :].astype(BF16)
        for c in range(2):
            kj = k_ref[pl.ds(off, blk), c * d:(c + 1) * d].astype(BF16)
            s = _dot_nt(q[c], kj) * scale + bias
            if mask is not None:
                s = jnp.where(mask, s, NEG)
            p, alpha, m_new, l_new = _softmax_step(s, m_ref[c], l_ref[c])
            m_ref[c] = m_new
            l_ref[c] = l_new
            acc_ref[c] = alpha * acc_ref[c] + jnp.dot(p.astype(BF16), vj, preferred_element_type=F32)

    def far_body(j, carry):
        step(j, far_ref[0], None)
        return carry

    lax.fori_loop(0, jnp.maximum(i - 1, 0), far_body, 0)

    @pl.when(i >= 1)
    def _():
        step(i - 1, tab_ref[0, 1], None)

    r, c_io = _strict_lower(blk)
    step(i, tab_ref[0, 0], c_io <= r)

    lam = _lambda(lv_ref)
    out = acc_ref[0] / l_ref[0] - lam * (acc_ref[1] / l_ref[1])
    ms = jnp.mean(out * out, axis=-1, keepdims=True)
    o_ref[...] = (out * lax.rsqrt(ms + EPS) * g_ref[...] * (1.0 - LAMBDA_INIT)).astype(o_ref.dtype)


def _dc_prompt(q, k, v, B, T, H, rel_bias, lam_vecs, sub_g, out_dtype):
    blk = LANES if T % LANES == 0 else T
    nq = T // blk
    d2 = 2 * LANES
    nb = rel_bias.shape[0]
    r = jnp.arange(blk, dtype=jnp.int32)
    rel0 = r[:, None] - r[None, :]
    tabs = jnp.stack([rel_bias[_t5_bucket(rel0, nb)], rel_bias[_t5_bucket(rel0 + blk, nb)]])
    tabs = jnp.transpose(tabs, (3, 0, 1, 2)).astype(F32)
    far = jnp.broadcast_to(rel_bias[nb - 1].astype(F32)[:, None, None], (H, 1, blk))
    assert blk >= MAX_DISTANCE or nq == 1
    kern = functools.partial(_dc_prompt_kernel, scale=LANES ** -0.5, blk=blk)
    return pl.pallas_call(
        kern,
        grid=(B, H, nq),
        in_specs=[
            pl.BlockSpec((blk, d2), lambda b, h, i: (b * nq + i, h)),
            pl.BlockSpec((T, d2), lambda b, h, i: (b, h)),
            pl.BlockSpec((T, d2), lambda b, h, i: (b, h)),
            pl.BlockSpec((1, 2, blk, blk), lambda b, h, i: (h, 0, 0, 0)),
            pl.BlockSpec((1, 1, blk), lambda b, h, i: (h, 0, 0)),
            pl.BlockSpec((4, LANES), lambda b, h, i: (0, 0)),
            pl.BlockSpec((1, d2), lambda b, h, i: (0, 0)),
        ],
        out_specs=pl.BlockSpec((blk, d2), lambda b, h, i: (b * nq + i, h)),
        out_shape=jax.ShapeDtypeStruct((B * T, H * d2), out_dtype),
        scratch_shapes=[pltpu.VMEM((2, blk, 1), F32), pltpu.VMEM((2, blk, 1), F32),
                        pltpu.VMEM((2, blk, d2), F32)],
        compiler_params=_cparams(("parallel", "parallel", "arbitrary")),
        name="dc_prompt",
    )(q, k, v, tabs, far, lam_vecs, sub_g.reshape(1, d2))


def _dc_sample_kernel(pt_ref, q_ref, kn_ref, vn_ref, kc_ref, vc_ref, bnew_ref, blast_ref, bfar_ref,
                      lv_ref, g_ref, o_ref, qbd_ref, m_ref, l_ref, acc_ref, *, scale, dec, groups):
    s = pl.program_id(1)
    ns = pl.num_programs(1)
    R = dec * groups
    W = acc_ref.shape[1]
    dshift = dec.bit_length() - 1
    lshift = LANES.bit_length() - 1

    @pl.when(s == 0)
    def _():
        q = q_ref[0]
        qrep = jnp.concatenate([q] * groups, axis=0)
        bd = _block_diag_mask(R, W, dshift, lshift)
        qbd_ref[...] = jnp.where(bd, qrep, 0.0).astype(BF16)
        m_ref[...] = jnp.full_like(m_ref, NEG)
        l_ref[...] = jnp.zeros_like(l_ref)
        acc_ref[...] = jnp.zeros_like(acc_ref)

    def process(k, v, bias, mask):
        sc = _dot_nt(qbd_ref[...], k.astype(BF16)) * scale + bias
        if mask is not None:
            sc = jnp.where(mask, sc, NEG)
        p, alpha, m_new, l_new = _softmax_step(sc, m_ref[...], l_ref[...])
        m_ref[...] = m_new
        l_ref[...] = l_new
        acc_ref[...] = alpha * acc_ref[...] + jnp.dot(p.astype(BF16), v.astype(BF16),
                                                      preferred_element_type=F32)

    @pl.when(s == 0)
    def _():
        rr = lax.broadcasted_iota(jnp.int32, (R, LANES), 0)
        cc = lax.broadcasted_iota(jnp.int32, (R, LANES), 1)
        process(kn_ref[0], vn_ref[0], bnew_ref[...], cc <= jnp.bitwise_and(rr, dec - 1))

    @pl.when(s == 1)
    def _():
        process(kc_ref[0], vc_ref[0], blast_ref[...], None)

    @pl.when(s > 1)
    def _():
        process(kc_ref[0], vc_ref[0], bfar_ref[...], None)

    @pl.when(s == ns - 1)
    def _():
        lam = _lambda(lv_ref)
        rr = lax.broadcasted_iota(jnp.int32, (R, 1), 0)
        comp = jnp.bitwise_and(jnp.right_shift(rr, dshift), 1)
        coef = jnp.where(comp == 0, 1.0, -lam) / l_ref[...]
        vmask = _block_diag_mask(R, W, dshift + 1, lshift + 1)
        out = _sum_row_groups(jnp.where(vmask, acc_ref[...] * coef, 0.0), dec)
        d2 = 2 * LANES
        for h in range(W // d2):
            x = out[:, h * d2:(h + 1) * d2]
            ms = jnp.mean(x * x, axis=-1, keepdims=True)
            o_ref[0, :, h * d2:(h + 1) * d2] = x * lax.rsqrt(ms + EPS) * g_ref[...] * (1.0 - LAMBDA_INIT)


def _dc_sample(q, k_new, v_new, cache_k, cache_v, page_table, Bs, Ts, rel_bias, lam_vecs, sub_g):
    W = q.shape[2]
    groups = W // LANES
    n_pages = page_table.shape[1]
    page = cache_k.shape[1]
    past = n_pages * page
    assert page == LANES and page >= MAX_DISTANCE and Ts & (Ts - 1) == 0
    nb = rel_bias.shape[0]
    R = Ts * groups
    rows = jnp.arange(R, dtype=jnp.int32)
    head = rows // (2 * Ts)
    tok = rows % Ts
    col = jnp.arange(page, dtype=jnp.int32)
    rb = rel_bias.astype(F32)
    b_new = rb[_t5_bucket(tok[:, None] - col[None, :], nb), head[:, None]]
    b_last = rb[_t5_bucket(past + tok[:, None] - ((n_pages - 1) * page + col[None, :]), nb), head[:, None]]
    b_far = jnp.broadcast_to(rb[nb - 1, head][:, None], (R, page))
    kern = functools.partial(_dc_sample_kernel, scale=LANES ** -0.5, dec=Ts, groups=groups)
    page_idx = lambda b, s, pt: (pt[b, n_pages - jnp.maximum(s, 1)], 0, 0)
    const = lambda b, s, pt: (0, 0)
    grid_spec = pltpu.PrefetchScalarGridSpec(
        num_scalar_prefetch=1,
        grid=(Bs, n_pages + 1),
        in_specs=[
            pl.BlockSpec((1, Ts, W), lambda b, s, pt: (b, 0, 0)),
            pl.BlockSpec((1, page, W), lambda b, s, pt: (b, 0, 0)),
            pl.BlockSpec((1, page, W), lambda b, s, pt: (b, 0, 0)),
            pl.BlockSpec((1, page, W), page_idx),
            pl.BlockSpec((1, page, W), page_idx),
            pl.BlockSpec((R, page), const),
            pl.BlockSpec((R, page), const),
            pl.BlockSpec((R, page), const),
            pl.BlockSpec((4, LANES), const),
            pl.BlockSpec((1, 2 * LANES), const),
        ],
        out_specs=pl.BlockSpec((1, Ts, W), lambda b, s, pt: (b, 0, 0)),
        scratch_shapes=[pltpu.VMEM((R, W), BF16), pltpu.VMEM((R, 1), F32), pltpu.VMEM((R, 1), F32),
                        pltpu.VMEM((R, W), F32)],
    )
    return pl.pallas_call(
        kern,
        grid_spec=grid_spec,
        out_shape=jax.ShapeDtypeStruct((Bs, Ts, W), F32),
        compiler_params=_cparams(("parallel", "arbitrary")),
        name="dc_sample",
    )(page_table, q, _pad_rows(k_new, page), _pad_rows(v_new, page), cache_k, cache_v,
      b_new, b_last, b_far, lam_vecs, sub_g.reshape(1, 2 * LANES))


def _layernorm(x, g, b):
    mu = jnp.mean(x, axis=-1, keepdims=True)
    xc = x - mu
    var = jnp.mean(xc * xc, axis=-1, keepdims=True)
    return xc * lax.rsqrt(var + EPS) * g + b


def _gmlp_prompt_kernel(u_ref, v_ref, lg_ref, lb_ref, ws_ref, bst_ref, d_ref, gv_ref, *, groups):
    vv = _layernorm(v_ref[...], lg_ref[...], lb_ref[...])
    gv_ref[0] = vv
    L = vv.shape[0]
    r, c = _strict_lower(L)
    tril = c <= r
    for g in range(groups):
        sl = slice(g * LANES, (g + 1) * LANES)
        w = jnp.where(tril, ws_ref[g], 0.0).astype(BF16)
        mixed = jnp.dot(w, vv[:, sl].astype(BF16), preferred_element_type=F32) + bst_ref[:, g:g + 1]
        d_ref[:, sl] = (u_ref[:, sl] * mixed).astype(d_ref.dtype)


def _gmlp_prompt(z, B, T, C, ln_g, ln_b, ws, bs, out_dtype):
    L = LANES
    nc = T // L
    groups = C // LANES
    kern = functools.partial(_gmlp_prompt_kernel, groups=groups)
    row = lambda b, c: (b * nc + c, 0)
    return pl.pallas_call(
        kern,
        grid=(B, nc),
        in_specs=[
            pl.BlockSpec((L, C), row),
            pl.BlockSpec((L, C), lambda b, c: (b * nc + c, 1)),
            pl.BlockSpec((1, C), lambda b, c: (0, 0)),
            pl.BlockSpec((1, C), lambda b, c: (0, 0)),
            pl.BlockSpec((groups, L, L), lambda b, c: (0, 0, 0)),
            pl.BlockSpec((L, groups), lambda b, c: (0, 0)),
        ],
        out_specs=[
            pl.BlockSpec((L, C), row),
            pl.BlockSpec((1, L, C), lambda b, c: (b, 0, 0)),
        ],
        out_shape=[
            jax.ShapeDtypeStruct((B * T, C), out_dtype),
            jax.ShapeDtypeStruct((B, L, C), F32),
        ],
        compiler_params=_cparams(("parallel", "arbitrary")),
        name="gmlp_prompt",
    )(z, z, ln_g.reshape(1, C), ln_b.reshape(1, C), ws[:, :L, :L], bs[:, :L].T)


def _gmlp_sample_kernel(u_ref, v_ref, lg_ref, lb_ref, wx_ref, bx_ref, d_ref, gv_ref):
    vv = _layernorm(v_ref[...], lg_ref[...], lb_ref[...])
    gv_ref[0] = vv
    L = vv.shape[0]
    mixed = bx_ref[...]
    for s in range(L):
        ws = wx_ref[s].astype(BF16).astype(F32)
        mixed = mixed + ws * vv[s:s + 1, :].astype(BF16).astype(F32)
    d_ref[...] = u_ref[...] * mixed


def _gmlp_sample(z, Bs, Ts, C, ln_g, ln_b, ws, bs):
    L = Ts
    groups = C // LANES
    w = jnp.tril(ws[:, :L, :L])
    wx = jnp.repeat(jnp.transpose(w, (2, 1, 0)), LANES, axis=2)
    bx = jnp.repeat(bs[:, :L].T, LANES, axis=1)
    row = lambda b: (b, 0)
    return pl.pallas_call(
        _gmlp_sample_kernel,
        grid=(Bs,),
        in_specs=[
            pl.BlockSpec((L, C), row),
            pl.BlockSpec((L, C), lambda b: (b, 1)),
            pl.BlockSpec((1, C), lambda b: (0, 0)),
            pl.BlockSpec((1, C), lambda b: (0, 0)),
            pl.BlockSpec((L, L, C), lambda b: (0, 0, 0)),
            pl.BlockSpec((L, C), lambda b: (0, 0)),
        ],
        out_specs=[
            pl.BlockSpec((L, C), row),
            pl.BlockSpec((1, L, C), lambda b: (b, 0, 0)),
        ],
        out_shape=[
            jax.ShapeDtypeStruct((Bs * Ts, C), F32),
            jax.ShapeDtypeStruct((Bs, L, C), F32),
        ],
        compiler_params=_cparams(("parallel",)),
        name="gmlp_sample",
    )(z, z, ln_g.reshape(1, C), ln_b.reshape(1, C), wx, bx)


def _run_group(x, mods, P, sample):
    B, T, D = x.shape
    M = B * T
    C = P["conv_w"].shape[1]
    sbw = P["sb_w"]
    dcw = P["dc_w"]
    gc = P["gmlp_ln_g"].shape[0]
    act = F32 if sample is not None else BF16
    big = sample is None

    def gate_args(g):
        if big:
            return dict(gate=g.reshape(B, 1, D), rows_per_gate=T)
        return dict(gate=jnp.repeat(g, T, axis=0))

    outs = {}
    for layer in range(2):
        sh1, sc1, gt1, sh2, sc2, gt2 = jnp.split(mods[layer], 6, axis=-1)
        h = _norm_mod(x, P["norm_g"][layer, 0], sc1, sh1, act).reshape(M, D)
        x2 = x.reshape(M, D)
        if layer == 0:
            w_in = P["w_in_even"]
            ag = _matmul(h, w_in, n_out=2 * C)
            q = _matmul(h, w_in, col_off=2 * C, n_out=sbw, out_dtype=act)
            k = _matmul(h, w_in, col_off=2 * C + sbw, n_out=sbw)
            v = _matmul(h, w_in, col_off=2 * C + 2 * sbw, n_out=sbw)
            heads = sbw // LANES
            if big:
                a_out, conv_st = _conv_module(ag, B, T, C, None, P["conv_w"], P["conv_b"],
                                              P["conv_ln_g"], P["conv_ln_b"], BF16)
                sb = _sb_prompt(q, k, v, B, T, heads, 0, BF16)
            else:
                a_out, conv_st = _conv_module(ag, B, T, C, sample["state_conv"], P["conv_w"],
                                              P["conv_b"], P["conv_ln_g"], P["conv_ln_b"], F32)
                sb = _sb_sample(q.reshape(B, T, sbw), k.reshape(B, T, sbw), v.reshape(B, T, sbw),
                                sample["cache_sb_k"], sample["cache_sb_v"], sample["page_table"],
                                B, T).reshape(M, sbw)
            outs["conv_st"] = conv_st
            outs["sb_k"] = k.reshape(B, T, heads, LANES)
            outs["sb_v"] = v.reshape(B, T, heads, LANES)
            mix = jnp.concatenate([a_out, sb], axis=-1)
            x2 = _matmul(mix, P["w_out_even"], res=x2, tn_pref=512, **gate_args(gt1))
        else:
            w_in = P["w_in_odd"]
            heads = dcw // (2 * LANES)
            qg = jnp.tile(P["q_norm_g"], dcw // LANES)
            kg = jnp.tile(P["k_norm_g"], dcw // LANES)
            q = _matmul(h, w_in, n_out=dcw, epilogue="group_rmsnorm", vec=qg, out_dtype=act)
            k = _matmul(h, w_in, col_off=dcw, n_out=dcw, epilogue="group_rmsnorm", vec=kg)
            v = _matmul(h, w_in, col_off=2 * dcw, n_out=dcw)
            z = _matmul(h, w_in, col_off=3 * dcw, n_out=2 * gc, epilogue="gelu")
            if big:
                c_out = _dc_prompt(q, k, v, B, T, heads, P["rel_bias"], P["lam_vecs"],
                                   P["sub_norm_g"], BF16)
                d_out, g_v = _gmlp_prompt(z, B, T, gc, P["gmlp_ln_g"], P["gmlp_ln_b"],
                                          P["gmlp_ws"], P["gmlp_bs"], BF16)
            else:
                c_out = _dc_sample(q.reshape(B, T, dcw), k.reshape(B, T, dcw), v.reshape(B, T, dcw),
                                   sample["cache_dc_k"], sample["cache_dc_v"], sample["page_table"],
                                   B, T, P["rel_bias"], P["lam_vecs"], P["sub_norm_g"]).reshape(M, dcw)
                d_out, g_v = _gmlp_sample(z, B, T, gc, P["gmlp_ln_g"], P["gmlp_ln_b"],
                                          P["gmlp_ws"], P["gmlp_bs"])
            outs["dc_k"] = k.reshape(B, T, heads, 2, LANES)
            outs["dc_v"] = v.reshape(B, T, heads, 2 * LANES)
            outs["g_v"] = g_v
            mix = jnp.concatenate([c_out, d_out], axis=-1)
            x2 = _matmul(mix, P["w_out_odd"], res=x2, tn_pref=512, **gate_args(gt1))
        h = _norm_mod(x2.reshape(B, T, D), P["norm_g"][layer, 1], sc2, sh2, act).reshape(M, D)
        a = _matmul(h, P["w_mlp_in"], layer=layer, epilogue="relu2", out_dtype=BF16)
        x2 = _matmul(a, P["w_mlp_out"], layer=layer, res=x2, tn_pref=512, **gate_args(gt2))
        x = x2.reshape(B, T, D)
    return x, outs


def kernel(x_prompt, x_sample, state_conv, cache_sb_k, cache_sb_v, cache_dc_k, cache_dc_v, page_table, c_prompt, c_sample, norm_g, w_ada, b_ada, w_in_even, w_out_even, conv_w, conv_b, conv_ln_g, conv_ln_b, w_in_odd, w_out_odd, q_norm_g, k_norm_g, lambda_q1, lambda_k1, lambda_q2, lambda_k2, sub_norm_g, rel_bias, gmlp_ln_g, gmlp_ln_b, gmlp_ws, gmlp_bs, w_mlp_in, w_mlp_out):
    B, T, D = x_prompt.shape
    Bs, Ts, _ = x_sample.shape
    n_pool, page = cache_sb_k.shape[:2]
    sbw = cache_sb_k.shape[2] * cache_sb_k.shape[3]
    dcw = cache_dc_v.shape[2] * cache_dc_v.shape[3]

    P = dict(
        norm_g=norm_g, conv_w=conv_w, conv_b=conv_b, conv_ln_g=conv_ln_g, conv_ln_b=conv_ln_b,
        q_norm_g=q_norm_g, k_norm_g=k_norm_g, sub_norm_g=sub_norm_g, rel_bias=rel_bias,
        gmlp_ln_g=gmlp_ln_g, gmlp_ln_b=gmlp_ln_b, gmlp_ws=gmlp_ws, gmlp_bs=gmlp_bs,
        lam_vecs=jnp.stack([lambda_q1, lambda_k1, lambda_q2, lambda_k2]).astype(F32),
        w_in_even=w_in_even.astype(BF16)[None], w_out_even=w_out_even.astype(BF16)[None],
        w_in_odd=w_in_odd.astype(BF16)[None], w_out_odd=w_out_odd.astype(BF16)[None],
        w_mlp_in=w_mlp_in.astype(BF16), w_mlp_out=w_mlp_out.astype(BF16),
        sb_w=sbw, dc_w=dcw,
    )

    nc = B + Bs
    rows = -(-nc // 16) * 16
    c_all = jnp.pad(jnp.concatenate([c_prompt, c_sample], axis=0), ((0, rows - nc), (0, 0)))
    mods = [_matmul(c_all, w_ada, layer=l, a_silu=True, epilogue="bias", vec=b_ada[l], tn_pref=512)
            for l in range(w_ada.shape[0])]
    mods_p = [m[:B] for m in mods]
    mods_s = [m[B:nc] for m in mods]

    y_p, o_p = _run_group(x_prompt, mods_p, P, None)
    sample = dict(
        state_conv=state_conv, page_table=page_table,
        cache_sb_k=cache_sb_k.reshape(n_pool, page, sbw), cache_sb_v=cache_sb_v.reshape(n_pool, page, sbw),
        cache_dc_k=cache_dc_k.reshape(n_pool, page, dcw), cache_dc_v=cache_dc_v.reshape(n_pool, page, dcw),
    )
    y_s, o_s = _run_group(x_sample, mods_s, P, sample)
    return (y_p, y_s, o_p["conv_st"], o_s["conv_st"],
            o_p["sb_k"], o_p["sb_v"], o_s["sb_k"], o_s["sb_v"],
            o_p["dc_k"], o_p["dc_v"], o_s["dc_k"], o_s["dc_v"],
            o_p["g_v"], o_s["g_v"])
```

```python
import functools
import math

import jax
import jax.numpy as jnp
from jax import lax
from jax.experimental import pallas as pl
from jax.experimental.pallas import tpu as pltpu

F32 = jnp.float32
BF16 = jnp.bfloat16

EPS = 1e-6
LAMBDA_INIT = 0.8 - 0.6 * math.exp(-0.3 * 1)
MAX_DISTANCE = 128
CONV_PAD = 32
LANES = 128
NEG = -1e30
SB_EXIT = -104.0

VMEM_LIMIT = 56 * 1024 * 1024


def _cparams(sem):
    return pltpu.CompilerParams(dimension_semantics=sem, vmem_limit_bytes=VMEM_LIMIT)


def _tile(dim, pref, align):
    if dim <= pref:
        return dim
    t = (pref // align) * align
    while t >= align:
        if dim % t == 0:
            return t
        t -= align
    return dim


def _norm_mod_kernel(x_ref, g_ref, sc_ref, sh_ref, o_ref):
    x = x_ref[0]
    ms = jnp.mean(x * x, axis=-1, keepdims=True)
    y = x * lax.rsqrt(ms + EPS) * g_ref[...]
    o_ref[0] = (y * (1.0 + sc_ref[0]) + sh_ref[0]).astype(o_ref.dtype)


def _norm_mod(x, g, sc, sh, out_dtype):
    B, T, D = x.shape
    tr = _tile(T, 256, 16)
    return pl.pallas_call(
        _norm_mod_kernel,
        grid=(B, T // tr),
        in_specs=[
            pl.BlockSpec((1, tr, D), lambda b, t: (b, t, 0)),
            pl.BlockSpec((1, D), lambda b, t: (0, 0)),
            pl.BlockSpec((1, 1, D), lambda b, t: (b, 0, 0)),
            pl.BlockSpec((1, 1, D), lambda b, t: (b, 0, 0)),
        ],
        out_specs=pl.BlockSpec((1, tr, D), lambda b, t: (b, t, 0)),
        out_shape=jax.ShapeDtypeStruct((B, T, D), out_dtype),
        compiler_params=_cparams(("parallel", "parallel")),
        name="norm_mod",
    )(x, g.reshape(1, D), sc.reshape(B, 1, D), sh.reshape(B, 1, D))


def _gelu_tanh(x):
    c = math.sqrt(2.0 / math.pi)
    return x * (0.5 * (1.0 + jnp.tanh(c * (x + 0.044715 * (x * x * x)))))


def _mm_kernel(*refs, nk, epilogue, a_silu, has_vec, has_res):
    it = iter(refs)
    a_ref = next(it)
    w_ref = next(it)
    vec_ref = next(it) if has_vec else None
    res_ref = next(it) if has_res else None
    gate_ref = next(it) if has_res else None
    o_ref = next(it)
    acc_ref = next(it) if nk > 1 else None

    a = a_ref[...]
    if a_silu:
        a = a * jax.nn.sigmoid(a)
    a = a.astype(BF16)
    w = w_ref[...].astype(BF16)
    part = jnp.dot(a, w, preferred_element_type=F32)

    def finish(acc):
        if epilogue == "relu2":
            r = jnp.maximum(acc, 0.0)
            out = r * r
        elif epilogue == "gelu":
            out = _gelu_tanh(acc)
        elif epilogue == "bias":
            out = acc + vec_ref[...]
        elif epilogue == "group_rmsnorm":
            tn = acc.shape[1]
            for g in range(tn // LANES):
                sl = slice(g * LANES, (g + 1) * LANES)
                x = acc[:, sl]
                ms = jnp.mean(x * x, axis=-1, keepdims=True)
                o_ref[:, sl] = (x * lax.rsqrt(ms + EPS) * vec_ref[:, sl]).astype(o_ref.dtype)
            return
        else:
            out = acc
        if has_res:
            gate = gate_ref[0] if len(gate_ref.shape) == 3 else gate_ref[...]
            out = res_ref[...] + gate * out
        o_ref[...] = out.astype(o_ref.dtype)

    if nk == 1:
        finish(part)
    else:
        k = pl.program_id(2)

        @pl.when(k == 0)
        def _():
            acc_ref[...] = part

        @pl.when(k > 0)
        def _():
            acc_ref[...] += part

        @pl.when(k == nk - 1)
        def _():
            finish(acc_ref[...])


def _matmul(a, w, *, layer=0, col_off=0, n_out=None, epilogue="none", a_silu=False,
            vec=None, res=None, gate=None, rows_per_gate=None, out_dtype=F32, tn_pref=1024):
    M, K = a.shape
    n_out = w.shape[2] - col_off if n_out is None else n_out
    tm = _tile(M if rows_per_gate is None else rows_per_gate, 1024, 16)
    tn = _tile(math.gcd(n_out, col_off), tn_pref, LANES)
    tk = _tile(K, 4096, LANES)
    assert col_off % tn == 0
    nk = K // tk
    joff = col_off // tn
    grid = (M // tm, n_out // tn, nk)

    in_specs = [
        pl.BlockSpec((tm, tk), lambda i, j, k: (i, k)),
        pl.BlockSpec((None, tk, tn), lambda i, j, k: (layer, k, j + joff)),
    ]
    args = [a, w]
    if vec is not None:
        in_specs.append(pl.BlockSpec((1, tn), lambda i, j, k: (0, j)))
        args.append(vec.reshape(1, n_out))
    if res is not None:
        in_specs.append(pl.BlockSpec((tm, tn), lambda i, j, k: (i, j)))
        args.append(res)
        if gate.ndim == 3:
            assert rows_per_gate % tm == 0
            rpg = rows_per_gate // tm
            in_specs.append(pl.BlockSpec((1, 1, tn), lambda i, j, k: (i // rpg, 0, j)))
        else:
            in_specs.append(pl.BlockSpec((tm, tn), lambda i, j, k: (i, j)))
        args.append(gate)
    scratch = [pltpu.VMEM((tm, tn), F32)] if nk > 1 else []
    kern = functools.partial(_mm_kernel, nk=nk, epilogue=epilogue, a_silu=a_silu,
                             has_vec=vec is not None, has_res=res is not None)
    return pl.pallas_call(
        kern,
        grid=grid,
        in_specs=in_specs,
        out_specs=pl.BlockSpec((tm, tn), lambda i, j, k: (i, j)),
        out_shape=jax.ShapeDtypeStruct((M, n_out), out_dtype),
        scratch_shapes=scratch,
        compiler_params=_cparams(("parallel", "parallel", "arbitrary")),
        name="mm_" + epilogue,
    )(*args)


def _conv_kernel(val_ref, gate_ref, pre_ref, w_ref, b_ref, lg_ref, lb_ref, o_ref, st_ref,
                 xx_ref, cv_ref, *, tr, width, rc, cc):
    t = pl.program_id(1)
    nt = pl.num_programs(1)
    C = val_ref.shape[1]

    @pl.when(t == 0)
    def _():
        xx_ref[0:CONV_PAD, :] = pre_ref[0]

    xx_ref[CONV_PAD:CONV_PAD + tr, :] = val_ref[...] * jax.nn.sigmoid(gate_ref[...])

    base = CONV_PAD - (width - 1)
    for r0 in range(0, tr, rc):
        for c0 in range(0, C, cc):
            acc = jnp.zeros((rc, cc), F32)
            for k in range(width):
                acc = acc + w_ref[k:k + 1, c0:c0 + cc] * xx_ref[base + r0 + k:base + r0 + k + rc, c0:c0 + cc]
            cv_ref[r0:r0 + rc, c0:c0 + cc] = acc + b_ref[:, c0:c0 + cc]

    cv = cv_ref[...]
    mu = jnp.mean(cv, axis=-1, keepdims=True)
    xc = cv - mu
    var = jnp.mean(xc * xc, axis=-1, keepdims=True)
    y = xc * lax.rsqrt(var + EPS) * lg_ref[...] + lb_ref[...]
    o_ref[...] = (y * jax.nn.sigmoid(y)).astype(o_ref.dtype)

    @pl.when(t == nt - 1)
    def _():
        st_ref[0] = xx_ref[tr:tr + CONV_PAD, :]

    @pl.when(t < nt - 1)
    def _():
        xx_ref[0:CONV_PAD, :] = xx_ref[tr:tr + CONV_PAD, :]


def _conv_module(proj, B, T, C, prefix, conv_w, conv_b, ln_g, ln_b, out_dtype):
    width = conv_w.shape[0]
    hist = width - 1
    assert hist <= CONV_PAD
    tr = _tile(T, 256, 8)
    nt = T // tr
    assert nt == 1 or tr >= CONV_PAD
    if prefix is None:
        pre = jnp.zeros((B, CONV_PAD, C), F32)
    else:
        pre = jnp.pad(prefix.astype(F32), ((0, 0), (CONV_PAD - hist, 0), (0, 0)))
    rc = _tile(tr, 32, 8)
    cc = _tile(C, 512, LANES)
    kern = functools.partial(_conv_kernel, tr=tr, width=width, rc=rc, cc=cc)
    row = lambda b, t: (b * nt + t, 0)
    a_out, st = pl.pallas_call(
        kern,
        grid=(B, nt),
        in_specs=[
            pl.BlockSpec((tr, C), row),
            pl.BlockSpec((tr, C), lambda b, t: (b * nt + t, 1)),
            pl.BlockSpec((1, CONV_PAD, C), lambda b, t: (b, 0, 0)),
            pl.BlockSpec((width, C), lambda b, t: (0, 0)),
            pl.BlockSpec((1, C), lambda b, t: (0, 0)),
            pl.BlockSpec((1, C), lambda b, t: (0, 0)),
            pl.BlockSpec((1, C), lambda b, t: (0, 0)),
        ],
        out_specs=[
            pl.BlockSpec((tr, C), row),
            pl.BlockSpec((1, CONV_PAD, C), lambda b, t: (b, 0, 0)),
        ],
        out_shape=[
            jax.ShapeDtypeStruct((B * T, C), out_dtype),
            jax.ShapeDtypeStruct((B, CONV_PAD, C), F32),
        ],
        scratch_shapes=[pltpu.VMEM((CONV_PAD + tr, C), F32), pltpu.VMEM((tr, C), F32)],
        compiler_params=_cparams(("parallel", "arbitrary")),
        name="conv_module",
    )(proj, proj, pre, conv_w, conv_b.reshape(1, C), ln_g.reshape(1, C), ln_b.reshape(1, C))
    return a_out, st[:, CONV_PAD - hist:, :]


def _dot_nt(a, b):
    return lax.dot_general(a, b, (((1,), (1,)), ((), ())), preferred_element_type=F32)


def _iota2(rows, cols=None):
    shape = (rows, rows if cols is None else cols)
    return lax.broadcasted_iota(jnp.int32, shape, 0), lax.broadcasted_iota(jnp.int32, shape, 1)


def _sb_block(z, mask, c, tri):
    sp = jnp.maximum(z, 0.0) + jnp.log1p(jnp.exp(-jnp.abs(z)))
    ln = -sp if mask is None else jnp.where(mask, -sp, 0.0)
    hi = ln.astype(BF16)
    lo = (ln - hi.astype(F32)).astype(BF16)
    within = (jnp.dot(hi, tri, preferred_element_type=F32)
              + jnp.dot(lo, tri, preferred_element_type=F32))
    a = jnp.exp(z - sp + within + c)
    if mask is not None:
        a = jnp.where(mask, a, 0.0)
    return a, c + jnp.sum(ln, axis=-1, keepdims=True)


def _sb_prompt_kernel(q_ref, k_ref, v_ref, o_ref, c_ref, acc_ref, *, scale, blk, gh):
    i = pl.program_id(2)
    d = LANES
    r, c_io = _iota2(blk)
    tri = jnp.where(r > c_io, 1.0, 0.0).astype(BF16)
    qs = [q_ref[:, g * d:(g + 1) * d].astype(BF16) for g in range(gh)]
    c_ref[...] = jnp.zeros_like(c_ref)
    acc_ref[...] = jnp.zeros_like(acc_ref)

    def step(j, mask):
        off = pl.multiple_of(j * blk, blk)
        top = None
        for g in range(gh):
            kj = k_ref[pl.ds(off, blk), g * d:(g + 1) * d].astype(BF16)
            vj = v_ref[pl.ds(off, blk), g * d:(g + 1) * d].astype(BF16)
            z = _dot_nt(qs[g], kj) * scale
            a, c = _sb_block(z, mask, c_ref[g], tri)
            c_ref[g] = c
            acc_ref[g] += jnp.dot(a.astype(BF16), vj, preferred_element_type=F32)
            cmax = jnp.max(c)
            top = cmax if top is None else jnp.maximum(top, cmax)
        return top

    top = step(i, c_io < r)
    lax.while_loop(lambda jt: jnp.logical_and(jt[0] >= 0, jt[1] >= SB_EXIT),
                   lambda jt: (jt[0] - 1, step(jt[0], None)),
                   (i - 1, top))
    for g in range(gh):
        o_ref[:, g * d:(g + 1) * d] = acc_ref[g].astype(o_ref.dtype)


def _sb_prompt(q, k, v, B, T, H, out_dtype):
    blk = _tile(T, 256, LANES) if T % LANES == 0 else T
    nq = T // blk
    d = LANES
    gh = 2 if H % 2 == 0 else 1
    kern = functools.partial(_sb_prompt_kernel, scale=d ** -0.5, blk=blk, gh=gh)
    return pl.pallas_call(
        kern,
        grid=(B, H // gh, nq),
        in_specs=[
            pl.BlockSpec((blk, gh * d), lambda b, h, i: (b * nq + i, h)),
            pl.BlockSpec((T, gh * d), lambda b, h, i: (b, h)),
            pl.BlockSpec((T, gh * d), lambda b, h, i: (b, h)),
        ],
        out_specs=pl.BlockSpec((blk, gh * d), lambda b, h, i: (b * nq + i, h)),
        out_shape=jax.ShapeDtypeStruct((B * T, H * d), out_dtype),
        scratch_shapes=[pltpu.VMEM((gh, blk, 1), F32), pltpu.VMEM((gh, blk, d), F32)],
        compiler_params=_cparams(("parallel", "parallel", "arbitrary")),
        name="sb_prompt",
    )(q, k, v)


def _pad_rows(x, rows):
    return jnp.pad(x, ((0, 0), (0, rows - x.shape[1]), (0, 0)))


def _sb_sample_kernel(pt_ref, q_ref, kn_ref, vn_ref, kc_hbm, vc_hbm, o_ref,
                      kbuf, vbuf, sem, c_ref, acc_ref, *, scale, dec, heads, n_pages):
    b = pl.program_id(0)
    d = LANES
    R = dec * heads
    r, c_io = _iota2(LANES)
    tri = jnp.where(r > c_io, 1.0, 0.0).astype(BF16)
    q = q_ref[0]
    qs = [q[:, h * d:(h + 1) * d].astype(BF16) for h in range(heads)]

    def copies(p, slot):
        page = pt_ref[b, n_pages - 1 - p]
        return (pltpu.make_async_copy(kc_hbm.at[page], kbuf.at[slot], sem.at[0, slot]),
                pltpu.make_async_copy(vc_hbm.at[page], vbuf.at[slot], sem.at[1, slot]))

    def start(p, slot):
        for cp in copies(p, slot):
            cp.start()

    def wait(p, slot):
        for cp in copies(p, slot):
            cp.wait()

    start(0, 0)

    def process(load_k, load_v, mask):
        z = jnp.concatenate([_dot_nt(qs[h], load_k(h)) for h in range(heads)], axis=0) * scale
        a, c = _sb_block(z, mask, c_ref[...], tri)
        c_ref[...] = c
        pv = [jnp.dot(a[h * dec:(h + 1) * dec].astype(BF16), load_v(h), preferred_element_type=F32)
              for h in range(heads)]
        acc_ref[...] += jnp.concatenate(pv, axis=0)
        return jnp.max(c)

    c_ref[...] = jnp.zeros_like(c_ref)
    acc_ref[...] = jnp.zeros_like(acc_ref)
    rr, cc = _iota2(R, LANES)
    top = process(lambda h: kn_ref[0, :, h * d:(h + 1) * d].astype(BF16),
                  lambda h: vn_ref[0, :, h * d:(h + 1) * d].astype(BF16),
                  cc < jnp.bitwise_and(rr, dec - 1))

    def body(carry):
        p, _ = carry
        slot = jnp.bitwise_and(p, 1)
        wait(p, slot)

        @pl.when(p + 1 < n_pages)
        def _():
            start(p + 1, 1 - slot)

        top = process(lambda h: kbuf[slot, pl.ds(h, LANES, stride=heads), :].astype(BF16),
                      lambda h: vbuf[slot, pl.ds(h, LANES, stride=heads), :].astype(BF16),
                      None)
        return p + 1, top

    p_end, _ = lax.while_loop(lambda pt: jnp.logical_and(pt[0] < n_pages, pt[1] >= SB_EXIT),
                              body, (jnp.int32(0), top))

    @pl.when(p_end < n_pages)
    def _():
        wait(p_end, jnp.bitwise_and(p_end, 1))

    acc = acc_ref[...]
    for h in range(heads):
        o_ref[0, :, h * d:(h + 1) * d] = acc[h * dec:(h + 1) * dec]


def _sb_sample(q, k_new, v_new, cache_k, cache_v, page_table):
    Bs, Ts, W = q.shape
    n_pool, page, heads, d = cache_k.shape
    n_pages = page_table.shape[1]
    assert page == LANES and d == LANES and Ts & (Ts - 1) == 0
    R = Ts * heads
    kern = functools.partial(_sb_sample_kernel, scale=d ** -0.5, dec=Ts, heads=heads, n_pages=n_pages)
    per_b = lambda b, pt: (b, 0, 0)
    grid_spec = pltpu.PrefetchScalarGridSpec(
        num_scalar_prefetch=1,
        grid=(Bs,),
        in_specs=[
            pl.BlockSpec((1, Ts, W), per_b),
            pl.BlockSpec((1, page, W), per_b),
            pl.BlockSpec((1, page, W), per_b),
            pl.BlockSpec(memory_space=pl.ANY),
            pl.BlockSpec(memory_space=pl.ANY),
        ],
        out_specs=pl.BlockSpec((1, Ts, W), per_b),
        scratch_shapes=[
            pltpu.VMEM((2, page * heads, d), F32), pltpu.VMEM((2, page * heads, d), F32),
            pltpu.SemaphoreType.DMA((2, 2)),
            pltpu.VMEM((R, 1), F32), pltpu.VMEM((R, d), F32),
        ],
    )
    return pl.pallas_call(
        kern,
        grid_spec=grid_spec,
        out_shape=jax.ShapeDtypeStruct((Bs, Ts, W), F32),
        compiler_params=_cparams(("arbitrary",)),
        name="sb_sample",
    )(page_table, q, _pad_rows(k_new, page), _pad_rows(v_new, page),
      cache_k.reshape(n_pool, page * heads, d), cache_v.reshape(n_pool, page * heads, d))


def _t5_bucket(rel, n_buckets):
    n = jnp.maximum(rel, 0)
    max_exact = n_buckets // 2
    nf = jnp.maximum(n, 1).astype(F32)
    large = max_exact + (jnp.log(nf / max_exact) / math.log(MAX_DISTANCE / max_exact)
                         * (n_buckets - max_exact)).astype(jnp.int32)
    large = jnp.minimum(large, n_buckets - 1)
    return jnp.where(n < max_exact, n, large)


def _bias_table(rel, rel_bias):
    nb = rel_bias.shape[0]
    onehot = jax.nn.one_hot(_t5_bucket(rel, nb), nb, dtype=F32)
    return jnp.einsum("...n,nh->h...", onehot, rel_bias.astype(F32), precision=lax.Precision.HIGHEST)


def _lambda(lv_ref):
    s1 = jnp.sum(lv_ref[0:1, :] * lv_ref[1:2, :], axis=-1, keepdims=True)
    s2 = jnp.sum(lv_ref[2:3, :] * lv_ref[3:4, :], axis=-1, keepdims=True)
    return jnp.exp(s1) - jnp.exp(s2) + LAMBDA_INIT


def _softmax_step(s, m, l):
    m_new = jnp.maximum(m, jnp.max(s, axis=-1, keepdims=True))
    alpha = jnp.exp(m - m_new)
    p = jnp.exp(s - m_new)
    return p, alpha, m_new, alpha * l + jnp.sum(p, axis=-1, keepdims=True)


def _sub_norm(x, g):
    ms = jnp.mean(x * x, axis=-1, keepdims=True)
    return x * lax.rsqrt(ms + EPS) * g * (1.0 - LAMBDA_INIT)


def _dc_prompt_kernel(q_ref, k_ref, v_ref, tab_ref, far_ref, lv_ref, g_ref, o_ref,
                      m_ref, l_ref, acc_ref, *, scale, blk):
    i = pl.program_id(2)
    d = LANES
    q = [q_ref[:, c * d:(c + 1) * d].astype(BF16) for c in range(2)]
    m_ref[...] = jnp.full_like(m_ref, NEG)
    l_ref[...] = jnp.zeros_like(l_ref)
    acc_ref[...] = jnp.zeros_like(acc_ref)

    def step(j, bias, mask):
        off = pl.multiple_of(j * blk, blk)
        vj = v_ref[pl.ds(off, blk), :].astype(BF16)
        for c in range(2):
            kj = k_ref[pl.ds(off, blk), c * d:(c + 1) * d].astype(BF16)
            s = _dot_nt(q[c], kj) * scale + bias
            if mask is not None:
                s = jnp.where(mask, s, NEG)
            p, alpha, m_new, l_new = _softmax_step(s, m_ref[c], l_ref[c])
            m_ref[c] = m_new
            l_ref[c] = l_new
            acc_ref[c] = alpha * acc_ref[c] + jnp.dot(p.astype(BF16), vj, preferred_element_type=F32)

    def far_body(j, carry):
        step(j, far_ref[0], None)
        return carry

    lax.fori_loop(0, jnp.maximum(i - 1, 0), far_body, 0)

    @pl.when(i >= 1)
    def _():
        step(i - 1, tab_ref[0, 1], None)

    r, c_io = _iota2(blk)
    step(i, tab_ref[0, 0], c_io <= r)

    lam = _lambda(lv_ref)
    out = acc_ref[0] / l_ref[0] - lam * (acc_ref[1] / l_ref[1])
    o_ref[...] = _sub_norm(out, g_ref[...]).astype(o_ref.dtype)


def _dc_prompt(q, k, v, B, T, H, rel_bias, lam_vecs, sub_g, out_dtype):
    blk = _tile(T, 256, LANES) if T % LANES == 0 else T
    nq = T // blk
    d2 = 2 * LANES
    assert blk >= MAX_DISTANCE or nq == 1
    r = jnp.arange(blk, dtype=jnp.int32)
    rel0 = r[:, None] - r[None, :]
    tabs = jnp.stack([_bias_table(rel0, rel_bias), _bias_table(rel0 + blk, rel_bias)], axis=1)
    far = jnp.broadcast_to(rel_bias[-1].astype(F32)[:, None, None], (H, 1, blk))
    kern = functools.partial(_dc_prompt_kernel, scale=LANES ** -0.5, blk=blk)
    return pl.pallas_call(
        kern,
        grid=(B, H, nq),
        in_specs=[
            pl.BlockSpec((blk, d2), lambda b, h, i: (b * nq + i, h)),
            pl.BlockSpec((T, d2), lambda b, h, i: (b, h)),
            pl.BlockSpec((T, d2), lambda b, h, i: (b, h)),
            pl.BlockSpec((1, 2, blk, blk), lambda b, h, i: (h, 0, 0, 0)),
            pl.BlockSpec((1, 1, blk), lambda b, h, i: (h, 0, 0)),
            pl.BlockSpec((4, LANES), lambda b, h, i: (0, 0)),
            pl.BlockSpec((1, d2), lambda b, h, i: (0, 0)),
        ],
        out_specs=pl.BlockSpec((blk, d2), lambda b, h, i: (b * nq + i, h)),
        out_shape=jax.ShapeDtypeStruct((B * T, H * d2), out_dtype),
        scratch_shapes=[pltpu.VMEM((2, blk, 1), F32), pltpu.VMEM((2, blk, 1), F32),
                        pltpu.VMEM((2, blk, d2), F32)],
        compiler_params=_cparams(("parallel", "parallel", "arbitrary")),
        name="dc_prompt",
    )(q, k, v, tabs, far, lam_vecs, sub_g.reshape(1, d2))


def _dc_sample_kernel(pt_ref, q_ref, kn_ref, vn_ref, *rest, scale, dec, heads, n_pages, ppb):
    kc_refs = rest[:ppb]
    vc_refs = rest[ppb:2 * ppb]
    (bnew_ref, blast_ref, bfar_ref, lv_ref, g_ref, o_ref,
     m_ref, l_ref, acc_ref, spread_ref, own_ref) = rest[2 * ppb:]
    s = pl.program_id(1)
    ns = pl.num_programs(1)
    d = LANES
    d2 = 2 * LANES
    groups = 2 * heads
    R = dec * groups
    dshift = dec.bit_length() - 1
    hshift = heads.bit_length() - 1
    q = q_ref[0]
    qs = [q[:, g * d:(g + 1) * d].astype(BF16) for g in range(groups)]

    def process(load_k, pv_fn, bias, mask):
        sc = jnp.concatenate([_dot_nt(qs[g], load_k(g)) for g in range(groups)], axis=0) * scale + bias
        if mask is not None:
            sc = jnp.where(mask, sc, NEG)
        p, alpha, m_new, l_new = _softmax_step(sc, m_ref[...], l_ref[...])
        m_ref[...] = m_new
        l_ref[...] = l_new
        acc_ref[...] = alpha * acc_ref[...] + pv_fn(p)

    @pl.when(s == 0)
    def _():
        m_ref[...] = jnp.full_like(m_ref, NEG)
        l_ref[...] = jnp.zeros_like(l_ref)
        acc_ref[...] = jnp.zeros_like(acc_ref)
        kk, jj = _iota2(LANES, LANES * heads)
        spread_ref[...] = jnp.where(jnp.right_shift(jj, hshift) == kk, 1.0, 0.0).astype(BF16)
        rr, jj = _iota2(R, LANES * heads)
        own_ref[...] = jnp.where(jnp.right_shift(rr, dshift + 1) == jnp.bitwise_and(jj, heads - 1), 1.0, 0.0)

        def pv_new(p):
            return jnp.concatenate(
                [jnp.dot(p[h * 2 * dec:(h + 1) * 2 * dec].astype(BF16),
                         vn_ref[0, :, h * d2:(h + 1) * d2].astype(BF16), preferred_element_type=F32)
                 for h in range(heads)], axis=0)

        rr, cc = _iota2(R, LANES)
        process(lambda g: kn_ref[0, :, g * d:(g + 1) * d].astype(BF16), pv_new,
                bnew_ref[...], cc <= jnp.bitwise_and(rr, dec - 1))

    for u in range(ppb):
        def pv_page(p, u=u):
            pe = jnp.dot(p.astype(BF16), spread_ref[...], preferred_element_type=F32) * own_ref[...]
            return jnp.dot(pe.astype(BF16), vc_refs[u][0].astype(BF16), preferred_element_type=F32)

        bias = jnp.where(s * ppb + u == n_pages - 1, blast_ref[...], bfar_ref[...])
        process(lambda g, u=u: kc_refs[u][0, pl.ds(g, LANES, stride=groups), :].astype(BF16),
                pv_page, bias, None)

    @pl.when(s == ns - 1)
    def _():
        lam = _lambda(lv_ref)
        on = acc_ref[...] / l_ref[...]
        for h in range(heads):
            r0 = h * 2 * dec
            out = on[r0:r0 + dec] - lam * on[r0 + dec:r0 + 2 * dec]
            o_ref[0, :, h * d2:(h + 1) * d2] = _sub_norm(out, g_ref[...])


def _dc_sample(q, k_new, v_new, cache_k, cache_v, page_table, rel_bias, lam_vecs, sub_g):
    Bs, Ts, W = q.shape
    n_pool, page, heads, d2 = cache_v.shape
    n_pages = page_table.shape[1]
    past = n_pages * page
    assert page == LANES and page >= MAX_DISTANCE and d2 == 2 * LANES
    assert Ts & (Ts - 1) == 0 and heads & (heads - 1) == 0
    ppb = 4 if n_pages % 4 == 0 else 1
    R = Ts * 2 * heads
    tok = jnp.arange(Ts, dtype=jnp.int32)[:, None]
    col = jnp.arange(page, dtype=jnp.int32)[None, :]
    rows = lambda t: jnp.broadcast_to(t[:, None], (heads, 2, Ts, page)).reshape(R, page)
    b_new = rows(_bias_table(tok - col, rel_bias))
    b_last = rows(_bias_table(past + tok - ((n_pages - 1) * page + col), rel_bias))
    b_far = rows(jnp.broadcast_to(rel_bias[-1].astype(F32)[:, None, None], (heads, Ts, page)))
    kern = functools.partial(_dc_sample_kernel, scale=LANES ** -0.5, dec=Ts, heads=heads,
                             n_pages=n_pages, ppb=ppb)
    per_b = lambda b, s, pt: (b, 0, 0)
    const = lambda b, s, pt: (0, 0)
    page_spec = lambda rows_, cols_, u: pl.BlockSpec(
        (1, rows_, cols_), lambda b, s, pt: (pt[b, s * ppb + u], 0, 0))
    grid_spec = pltpu.PrefetchScalarGridSpec(
        num_scalar_prefetch=1,
        grid=(Bs, n_pages // ppb),
        in_specs=[
            pl.BlockSpec((1, Ts, W), per_b),
            pl.BlockSpec((1, page, W), per_b),
            pl.BlockSpec((1, page, W), per_b),
            *[page_spec(page * heads * 2, LANES, u) for u in range(ppb)],
            *[page_spec(page * heads, d2, u) for u in range(ppb)],
            pl.BlockSpec((R, page), const),
            pl.BlockSpec((R, page), const),
            pl.BlockSpec((R, page), const),
            pl.BlockSpec((4, LANES), const),
            pl.BlockSpec((1, d2), const),
        ],
        out_specs=pl.BlockSpec((1, Ts, W), per_b),
        scratch_shapes=[pltpu.VMEM((R, 1), F32), pltpu.VMEM((R, 1), F32), pltpu.VMEM((R, d2), F32),
                        pltpu.VMEM((page, page * heads), BF16), pltpu.VMEM((R, page * heads), F32)],
    )
    ck = cache_k.reshape(n_pool, page * heads * 2, LANES)
    cv = cache_v.reshape(n_pool, page * heads, d2)
    return pl.pallas_call(
        kern,
        grid_spec=grid_spec,
        out_shape=jax.ShapeDtypeStruct((Bs, Ts, W), F32),
        compiler_params=_cparams(("parallel", "arbitrary")),
        name="dc_sample",
    )(page_table, q, _pad_rows(k_new, page), _pad_rows(v_new, page), *([ck] * ppb), *([cv] * ppb),
      b_new, b_last, b_far, lam_vecs, sub_g.reshape(1, d2))


def _layernorm(x, g, b):
    mu = jnp.mean(x, axis=-1, keepdims=True)
    xc = x - mu
    var = jnp.mean(xc * xc, axis=-1, keepdims=True)
    return xc * lax.rsqrt(var + EPS) * g + b


def _gmlp_prompt_kernel(u_ref, v_ref, lg_ref, lb_ref, ws_ref, bst_ref, d_ref, gv_ref, *, groups):
    vv = _layernorm(v_ref[...], lg_ref[...], lb_ref[...])
    gv_ref[0] = vv
    L = vv.shape[0]
    r, c = _iota2(L)
    tril = c <= r
    for g in range(groups):
        sl = slice(g * LANES, (g + 1) * LANES)
        w = jnp.where(tril, ws_ref[g], 0.0).astype(BF16)
        mixed = jnp.dot(w, vv[:, sl].astype(BF16), preferred_element_type=F32) + bst_ref[:, g:g + 1]
        d_ref[:, sl] = (u_ref[:, sl] * mixed).astype(d_ref.dtype)


def _gmlp_prompt(z, B, T, C, ln_g, ln_b, ws, bs, out_dtype):
    L = LANES
    nc = T // L
    groups = C // LANES
    kern = functools.partial(_gmlp_prompt_kernel, groups=groups)
    row = lambda b, c: (b * nc + c, 0)
    return pl.pallas_call(
        kern,
        grid=(B, nc),
        in_specs=[
            pl.BlockSpec((L, C), row),
            pl.BlockSpec((L, C), lambda b, c: (b * nc + c, 1)),
            pl.BlockSpec((1, C), lambda b, c: (0, 0)),
            pl.BlockSpec((1, C), lambda b, c: (0, 0)),
            pl.BlockSpec((groups, L, L), lambda b, c: (0, 0, 0)),
            pl.BlockSpec((L, groups), lambda b, c: (0, 0)),
        ],
        out_specs=[
            pl.BlockSpec((L, C), row),
            pl.BlockSpec((1, L, C), lambda b, c: (b, 0, 0)),
        ],
        out_shape=[
            jax.ShapeDtypeStruct((B * T, C), out_dtype),
            jax.ShapeDtypeStruct((B, L, C), F32),
        ],
        compiler_params=_cparams(("parallel", "arbitrary")),
        name="gmlp_prompt",
    )(z, z, ln_g.reshape(1, C), ln_b.reshape(1, C), ws[:, :L, :L], bs[:, :L].T)


def _gmlp_sample_kernel(u_ref, v_ref, lg_ref, lb_ref, wx_ref, bx_ref, d_ref, gv_ref):
    vv = _layernorm(v_ref[...], lg_ref[...], lb_ref[...])
    gv_ref[0] = vv
    L = vv.shape[0]
    mixed = bx_ref[...]
    for s in range(L):
        ws = wx_ref[s].astype(BF16).astype(F32)
        mixed = mixed + ws * vv[s:s + 1, :].astype(BF16).astype(F32)
    d_ref[...] = u_ref[...] * mixed


def _gmlp_sample(z, Bs, Ts, C, ln_g, ln_b, ws, bs):
    L = Ts
    w = jnp.tril(ws[:, :L, :L])
    wx = jnp.repeat(jnp.transpose(w, (2, 1, 0)), LANES, axis=2)
    bx = jnp.repeat(bs[:, :L].T, LANES, axis=1)
    row = lambda b: (b, 0)
    return pl.pallas_call(
        _gmlp_sample_kernel,
        grid=(Bs,),
        in_specs=[
            pl.BlockSpec((L, C), row),
            pl.BlockSpec((L, C), lambda b: (b, 1)),
            pl.BlockSpec((1, C), lambda b: (0, 0)),
            pl.BlockSpec((1, C), lambda b: (0, 0)),
            pl.BlockSpec((L, L, C), lambda b: (0, 0, 0)),
            pl.BlockSpec((L, C), lambda b: (0, 0)),
        ],
        out_specs=[
            pl.BlockSpec((L, C), row),
            pl.BlockSpec((1, L, C), lambda b: (b, 0, 0)),
        ],
        out_shape=[
            jax.ShapeDtypeStruct((Bs * Ts, C), F32),
            jax.ShapeDtypeStruct((Bs, L, C), F32),
        ],
        compiler_params=_cparams(("parallel",)),
        name="gmlp_sample",
    )(z, z, ln_g.reshape(1, C), ln_b.reshape(1, C), wx, bx)


def _run_group(x, mods, P, sample):
    B, T, D = x.shape
    M = B * T
    C = P["conv_w"].shape[1]
    sbw = P["sb_w"]
    dcw = P["dc_w"]
    gc = P["gmlp_ln_g"].shape[0]
    act = F32 if sample is not None else BF16
    big = sample is None

    def gate_args(g):
        if big:
            return dict(gate=g.reshape(B, 1, D), rows_per_gate=T)
        return dict(gate=jnp.repeat(g, T, axis=0))

    outs = {}
    for layer in range(2):
        sh1, sc1, gt1, sh2, sc2, gt2 = jnp.split(mods[layer], 6, axis=-1)
        h = _norm_mod(x, P["norm_g"][layer, 0], sc1, sh1, act).reshape(M, D)
        x2 = x.reshape(M, D)
        if layer == 0:
            w_in = P["w_in_even"]
            ag = _matmul(h, w_in, n_out=2 * C)
            q = _matmul(h, w_in, col_off=2 * C, n_out=sbw, out_dtype=act)
            k = _matmul(h, w_in, col_off=2 * C + sbw, n_out=sbw)
            v = _matmul(h, w_in, col_off=2 * C + 2 * sbw, n_out=sbw)
            heads = sbw // LANES
            if big:
                a_out, conv_st = _conv_module(ag, B, T, C, None, P["conv_w"], P["conv_b"],
                                              P["conv_ln_g"], P["conv_ln_b"], BF16)
                sb = _sb_prompt(q, k, v, B, T, heads, BF16)
            else:
                a_out, conv_st = _conv_module(ag, B, T, C, sample["state_conv"], P["conv_w"],
                                              P["conv_b"], P["conv_ln_g"], P["conv_ln_b"], F32)
                sb = _sb_sample(q.reshape(B, T, sbw), k.reshape(B, T, sbw), v.reshape(B, T, sbw),
                                sample["cache_sb_k"], sample["cache_sb_v"],
                                sample["page_table"]).reshape(M, sbw)
            outs["conv_st"] = conv_st
            outs["sb_k"] = k.reshape(B, T, heads, LANES)
            outs["sb_v"] = v.reshape(B, T, heads, LANES)
            mix = jnp.concatenate([a_out, sb], axis=-1)
            x2 = _matmul(mix, P["w_out_even"], res=x2, tn_pref=512, **gate_args(gt1))
        else:
            w_in = P["w_in_odd"]
            heads = dcw // (2 * LANES)
            qg = jnp.tile(P["q_norm_g"], dcw // LANES)
            kg = jnp.tile(P["k_norm_g"], dcw // LANES)
            q = _matmul(h, w_in, n_out=dcw, epilogue="group_rmsnorm", vec=qg, out_dtype=act)
            k = _matmul(h, w_in, col_off=dcw, n_out=dcw, epilogue="group_rmsnorm", vec=kg)
            v = _matmul(h, w_in, col_off=2 * dcw, n_out=dcw)
            z = _matmul(h, w_in, col_off=3 * dcw, n_out=2 * gc, epilogue="gelu")
            if big:
                c_out = _dc_prompt(q, k, v, B, T, heads, P["rel_bias"], P["lam_vecs"],
                                   P["sub_norm_g"], BF16)
                d_out, g_v = _gmlp_prompt(z, B, T, gc, P["gmlp_ln_g"], P["gmlp_ln_b"],
                                          P["gmlp_ws"], P["gmlp_bs"], BF16)
            else:
                c_out = _dc_sample(q.reshape(B, T, dcw), k.reshape(B, T, dcw), v.reshape(B, T, dcw),
                                   sample["cache_dc_k"], sample["cache_dc_v"], sample["page_table"],
                                   P["rel_bias"], P["lam_vecs"], P["sub_norm_g"]).reshape(M, dcw)
                d_out, g_v = _gmlp_sample(z, B, T, gc, P["gmlp_ln_g"], P["gmlp_ln_b"],
                                          P["gmlp_ws"], P["gmlp_bs"])
            outs["dc_k"] = k.reshape(B, T, heads, 2, LANES)
            outs["dc_v"] = v.reshape(B, T, heads, 2 * LANES)
            outs["g_v"] = g_v
            mix = jnp.concatenate([c_out, d_out], axis=-1)
            x2 = _matmul(mix, P["w_out_odd"], res=x2, tn_pref=512, **gate_args(gt1))
        h = _norm_mod(x2.reshape(B, T, D), P["norm_g"][layer, 1], sc2, sh2, act).reshape(M, D)
        a = _matmul(h, P["w_mlp_in"], layer=layer, epilogue="relu2", out_dtype=BF16)
        x2 = _matmul(a, P["w_mlp_out"], layer=layer, res=x2, tn_pref=512, **gate_args(gt2))
        x = x2.reshape(B, T, D)
    return x, outs


def kernel(x_prompt, x_sample, state_conv, cache_sb_k, cache_sb_v, cache_dc_k, cache_dc_v, page_table, c_prompt, c_sample, norm_g, w_ada, b_ada, w_in_even, w_out_even, conv_w, conv_b, conv_ln_g, conv_ln_b, w_in_odd, w_out_odd, q_norm_g, k_norm_g, lambda_q1, lambda_k1, lambda_q2, lambda_k2, sub_norm_g, rel_bias, gmlp_ln_g, gmlp_ln_b, gmlp_ws, gmlp_bs, w_mlp_in, w_mlp_out):
    B, T, D = x_prompt.shape
    Bs, Ts, _ = x_sample.shape
    sbw = cache_sb_k.shape[2] * cache_sb_k.shape[3]
    dcw = cache_dc_v.shape[2] * cache_dc_v.shape[3]

    P = dict(
        norm_g=norm_g, conv_w=conv_w, conv_b=conv_b, conv_ln_g=conv_ln_g, conv_ln_b=conv_ln_b,
        q_norm_g=q_norm_g, k_norm_g=k_norm_g, sub_norm_g=sub_norm_g, rel_bias=rel_bias,
        gmlp_ln_g=gmlp_ln_g, gmlp_ln_b=gmlp_ln_b, gmlp_ws=gmlp_ws, gmlp_bs=gmlp_bs,
        lam_vecs=jnp.stack([lambda_q1, lambda_k1, lambda_q2, lambda_k2]).astype(F32),
        w_in_even=w_in_even.astype(BF16)[None], w_out_even=w_out_even.astype(BF16)[None],
        w_in_odd=w_in_odd.astype(BF16)[None], w_out_odd=w_out_odd.astype(BF16)[None],
        w_mlp_in=w_mlp_in.astype(BF16), w_mlp_out=w_mlp_out.astype(BF16),
        sb_w=sbw, dc_w=dcw,
    )

    nc = B + Bs
    rows = -(-nc // 16) * 16
    c_all = jnp.pad(jnp.concatenate([c_prompt, c_sample], axis=0), ((0, rows - nc), (0, 0)))
    mods = [_matmul(c_all, w_ada, layer=l, a_silu=True, epilogue="bias", vec=b_ada[l], tn_pref=512)
            for l in range(w_ada.shape[0])]
    mods_p = [m[:B] for m in mods]
    mods_s = [m[B:nc] for m in mods]

    y_p, o_p = _run_group(x_prompt, mods_p, P, None)
    sample = dict(state_conv=state_conv, page_table=page_table, cache_sb_k=cache_sb_k,
                  cache_sb_v=cache_sb_v, cache_dc_k=cache_dc_k, cache_dc_v=cache_dc_v)
    y_s, o_s = _run_group(x_sample, mods_s, P, sample)
    return (y_p, y_s, o_p["conv_st"], o_s["conv_st"],
            o_p["sb_k"], o_p["sb_v"], o_s["sb_k"], o_s["sb_v"],
            o_p["dc_k"], o_p["dc_v"], o_s["dc_k"], o_s["dc_v"],
            o_p["g_v"], o_s["g_v"])
```

```python
import functools
import math

import jax
import jax.numpy as jnp
from jax import lax
from jax.experimental import pallas as pl
from jax.experimental.pallas import tpu as pltpu

F32 = jnp.float32
BF16 = jnp.bfloat16

EPS = 1e-6
LAMBDA_INIT = 0.8 - 0.6 * math.exp(-0.3 * 1)
MAX_DISTANCE = 128
CONV_PAD = 32
LANES = 128
NEG = -1e30
SB_EXIT = -104.0

VMEM_LIMIT = 56 * 1024 * 1024


def _cparams(sem):
    return pltpu.CompilerParams(dimension_semantics=sem, vmem_limit_bytes=VMEM_LIMIT)


def _tile(dim, pref, align):
    if dim <= pref:
        return dim
    t = (pref // align) * align
    while t >= align:
        if dim % t == 0:
            return t
        t -= align
    return dim


def _norm_mod_kernel(x_ref, g_ref, sc_ref, sh_ref, o_ref):
    x = x_ref[0]
    ms = jnp.mean(x * x, axis=-1, keepdims=True)
    y = x * lax.rsqrt(ms + EPS) * g_ref[...]
    o_ref[0] = (y * (1.0 + sc_ref[0]) + sh_ref[0]).astype(o_ref.dtype)


def _norm_mod(x, g, sc, sh, out_dtype):
    B, T, D = x.shape
    tr = _tile(T, 256, 16)
    return pl.pallas_call(
        _norm_mod_kernel,
        grid=(B, T // tr),
        in_specs=[
            pl.BlockSpec((1, tr, D), lambda b, t: (b, t, 0)),
            pl.BlockSpec((1, D), lambda b, t: (0, 0)),
            pl.BlockSpec((1, 1, D), lambda b, t: (b, 0, 0)),
            pl.BlockSpec((1, 1, D), lambda b, t: (b, 0, 0)),
        ],
        out_specs=pl.BlockSpec((1, tr, D), lambda b, t: (b, t, 0)),
        out_shape=jax.ShapeDtypeStruct((B, T, D), out_dtype),
        compiler_params=_cparams(("parallel", "parallel")),
        name="norm_mod",
    )(x, g.reshape(1, D), sc.reshape(B, 1, D), sh.reshape(B, 1, D))


def _gelu_tanh(x):
    c = math.sqrt(2.0 / math.pi)
    return x * (0.5 * (1.0 + jnp.tanh(c * (x + 0.044715 * (x * x * x)))))


def _mm_kernel(*refs, nk, epilogue, a_silu, has_vec, has_res, emit_w):
    it = iter(refs)
    a_ref = next(it)
    w_ref = next(it)
    vec_ref = next(it) if has_vec else None
    res_ref = next(it) if has_res else None
    gate_ref = next(it) if has_res else None
    o_ref = next(it)
    wb_ref = next(it) if emit_w else None
    acc_ref = next(it) if nk > 1 else None

    if emit_w:
        @pl.when(pl.program_id(1) == 0)
        def _():
            wb_ref[...] = w_ref[...].astype(BF16)

        w = wb_ref[...]
    else:
        w = w_ref[...].astype(BF16)
    a = a_ref[...]
    if a_silu:
        a = a * jax.nn.sigmoid(a)
    part = jnp.dot(a.astype(BF16), w, preferred_element_type=F32)

    def finish(acc):
        if epilogue == "relu2":
            r = jnp.maximum(acc, 0.0)
            out = r * r
        elif epilogue == "gelu":
            out = _gelu_tanh(acc)
        elif epilogue == "bias":
            out = acc + vec_ref[...]
        elif epilogue == "group_rmsnorm":
            tn = acc.shape[1]
            for g in range(tn // LANES):
                sl = slice(g * LANES, (g + 1) * LANES)
                x = acc[:, sl]
                ms = jnp.mean(x * x, axis=-1, keepdims=True)
                o_ref[:, sl] = (x * lax.rsqrt(ms + EPS) * vec_ref[:, sl]).astype(o_ref.dtype)
            return
        else:
            out = acc
        if has_res:
            gate = gate_ref[0] if len(gate_ref.shape) == 3 else gate_ref[...]
            out = res_ref[...] + gate * out
        o_ref[...] = out.astype(o_ref.dtype)

    if nk == 1:
        finish(part)
    else:
        k = pl.program_id(2)

        @pl.when(k == 0)
        def _():
            acc_ref[...] = part

        @pl.when(k > 0)
        def _():
            acc_ref[...] += part

        @pl.when(k == nk - 1)
        def _():
            finish(acc_ref[...])


def _matmul(a, w, *, layer=0, col_off=0, n_out=None, epilogue="none", a_silu=False,
            vec=None, res=None, gate=None, rows_per_gate=None, out_dtype=F32, tn_pref=1024,
            emit_w=False):
    M, K = a.shape
    n_out = w.shape[2] - col_off if n_out is None else n_out
    small = M <= 256
    tm = _tile(M if rows_per_gate is None else rows_per_gate, 1024, 16)
    tk = _tile(K, 4096 if K <= 4096 else 2048, LANES)
    nk = K // tk
    if emit_w or small or (res is not None and nk == 1):
        tn_pref = min(tn_pref, 512)
    tn = _tile(math.gcd(n_out, col_off), tn_pref, LANES)
    assert col_off % tn == 0
    joff = col_off // tn
    if emit_w:
        assert nk == 1 and w.dtype == F32
        grid = (n_out // tn, M // tm)
        ids = lambda j, i: (i, j, 0)
        sem = ("parallel", "arbitrary")
    else:
        grid = (M // tm, n_out // tn, nk)
        ids = lambda i, j, k: (i, j, k)
        sem = ("parallel", "parallel", "arbitrary")

    def spec(shape, fn):
        return pl.BlockSpec(shape, lambda *g: fn(*ids(*g)))

    in_specs = [
        spec((tm, tk), lambda i, j, k: (i, k)),
        spec((None, tk, tn), lambda i, j, k: (layer, k, j + joff)),
    ]
    args = [a, w]
    if vec is not None:
        in_specs.append(spec((1, tn), lambda i, j, k: (0, j)))
        args.append(vec.reshape(1, n_out))
    if res is not None:
        in_specs.append(spec((tm, tn), lambda i, j, k: (i, j)))
        args.append(res)
        if gate.ndim == 3:
            assert rows_per_gate % tm == 0
            rpg = rows_per_gate // tm
            in_specs.append(spec((1, 1, tn), lambda i, j, k: (i // rpg, 0, j)))
        else:
            in_specs.append(spec((tm, tn), lambda i, j, k: (i, j)))
        args.append(gate)
    out_specs = [spec((tm, tn), lambda i, j, k: (i, j))]
    out_shape = [jax.ShapeDtypeStruct((M, n_out), out_dtype)]
    if emit_w:
        out_specs.append(spec((None, tk, tn), lambda i, j, k: (0, 0, j)))
        out_shape.append(jax.ShapeDtypeStruct((1, K, n_out), BF16))
    scratch = [pltpu.VMEM((tm, tn), F32)] if nk > 1 else []
    kern = functools.partial(_mm_kernel, nk=nk, epilogue=epilogue, a_silu=a_silu,
                             has_vec=vec is not None, has_res=res is not None, emit_w=emit_w)
    outs = pl.pallas_call(
        kern,
        grid=grid,
        in_specs=in_specs,
        out_specs=out_specs,
        out_shape=out_shape,
        scratch_shapes=scratch,
        compiler_params=_cparams(sem),
        name="mm_" + epilogue + ("_w" if emit_w else ""),
    )(*args)
    return tuple(outs) if emit_w else outs[0]


def _conv_kernel(val_ref, gate_ref, pre_ref, w_ref, b_ref, lg_ref, lb_ref, o_ref, st_ref,
                 xx_ref, cv_ref, *, tr, width, rc, cc):
    t = pl.program_id(1)
    nt = pl.num_programs(1)
    C = val_ref.shape[1]

    @pl.when(t == 0)
    def _():
        xx_ref[0:CONV_PAD, :] = pre_ref[0]

    xx_ref[CONV_PAD:CONV_PAD + tr, :] = val_ref[...] * jax.nn.sigmoid(gate_ref[...])

    base = CONV_PAD - (width - 1)
    sub = 8
    for r0 in range(0, tr, rc):
        for c0 in range(0, C, cc):
            acc = jnp.zeros((rc, cc), F32)
            for s in range(sub):
                rows = rc + (sub if s else 0)
                part = None
                for u in range(s, base + width, sub):
                    if u < base:
                        continue
                    term = (w_ref[u - base:u - base + 1, c0:c0 + cc]
                            * xx_ref[r0 + u - s:r0 + u - s + rows, c0:c0 + cc])
                    part = term if part is None else part + term
                acc = acc + part[s:s + rc]
            cv_ref[r0:r0 + rc, c0:c0 + cc] = acc + b_ref[:, c0:c0 + cc]

    cv = cv_ref[...]
    mu = jnp.mean(cv, axis=-1, keepdims=True)
    xc = cv - mu
    var = jnp.mean(xc * xc, axis=-1, keepdims=True)
    y = xc * lax.rsqrt(var + EPS) * lg_ref[...] + lb_ref[...]
    o_ref[...] = (y * jax.nn.sigmoid(y)).astype(o_ref.dtype)

    @pl.when(t == nt - 1)
    def _():
        st_ref[0] = xx_ref[tr:tr + CONV_PAD, :]

    @pl.when(t < nt - 1)
    def _():
        xx_ref[0:CONV_PAD, :] = xx_ref[tr:tr + CONV_PAD, :]


def _conv_module(proj, B, T, C, prefix, conv_w, conv_b, ln_g, ln_b, out_dtype):
    width = conv_w.shape[0]
    hist = width - 1
    assert hist <= CONV_PAD
    tr = _tile(T, 256, 8)
    nt = T // tr
    assert nt == 1 or tr >= CONV_PAD
    if prefix is None:
        pre = jnp.zeros((B, CONV_PAD, C), F32)
    else:
        pre = jnp.pad(prefix.astype(F32), ((0, 0), (CONV_PAD - hist, 0), (0, 0)))
    rc = _tile(tr, 32, 8)
    cc = _tile(C, 256, LANES)
    kern = functools.partial(_conv_kernel, tr=tr, width=width, rc=rc, cc=cc)
    row = lambda b, t: (b * nt + t, 0)
    a_out, st = pl.pallas_call(
        kern,
        grid=(B, nt),
        in_specs=[
            pl.BlockSpec((tr, C), row),
            pl.BlockSpec((tr, C), lambda b, t: (b * nt + t, 1)),
            pl.BlockSpec((1, CONV_PAD, C), lambda b, t: (b, 0, 0)),
            pl.BlockSpec((width, C), lambda b, t: (0, 0)),
            pl.BlockSpec((1, C), lambda b, t: (0, 0)),
            pl.BlockSpec((1, C), lambda b, t: (0, 0)),
            pl.BlockSpec((1, C), lambda b, t: (0, 0)),
        ],
        out_specs=[
            pl.BlockSpec((tr, C), row),
            pl.BlockSpec((1, CONV_PAD, C), lambda b, t: (b, 0, 0)),
        ],
        out_shape=[
            jax.ShapeDtypeStruct((B * T, C), out_dtype),
            jax.ShapeDtypeStruct((B, CONV_PAD, C), F32),
        ],
        scratch_shapes=[pltpu.VMEM((CONV_PAD + tr, C), F32), pltpu.VMEM((tr, C), F32)],
        compiler_params=_cparams(("parallel", "arbitrary")),
        name="conv_module",
    )(proj, proj, pre, conv_w, conv_b.reshape(1, C), ln_g.reshape(1, C), ln_b.reshape(1, C))
    return a_out, st[:, CONV_PAD - hist:, :]


def _dot_nt(a, b):
    return lax.dot_general(a, b, (((1,), (1,)), ((), ())), preferred_element_type=F32)


def _iota2(rows, cols=None):
    shape = (rows, rows if cols is None else cols)
    return lax.broadcasted_iota(jnp.int32, shape, 0), lax.broadcasted_iota(jnp.int32, shape, 1)


def _rep(x, n):
    return x if n == 1 else jnp.tile(x, (1, n))


def _tri_ones(kb):
    r, c = _iota2(kb)
    return jnp.where(r > c, 1.0, 0.0).astype(BF16), jnp.ones((kb, LANES), BF16)


def _sb_block(z, mask, c, tri, ones):
    sp = jnp.maximum(z, 0.0) + jnp.log1p(jnp.exp(-jnp.abs(z)))
    ln = -sp if mask is None else jnp.where(mask, -sp, 0.0)
    hi = ln.astype(BF16)
    lo = (ln - hi.astype(F32)).astype(BF16)
    within = (jnp.dot(hi, tri, preferred_element_type=F32)
              + jnp.dot(lo, tri, preferred_element_type=F32))
    total = (jnp.dot(hi, ones, preferred_element_type=F32)
             + jnp.dot(lo, ones, preferred_element_type=F32))
    a = jnp.exp(z - sp + within + _rep(c, z.shape[1] // LANES))
    if mask is not None:
        a = jnp.where(mask, a, 0.0)
    return a, c + total


def _sb_prompt_kernel(q_ref, k_ref, v_ref, o_ref, c_ref, acc_ref, *, scale, blk, gh):
    i = pl.program_id(2)
    d = LANES
    r, c_io = _iota2(blk)
    tri, ones = _tri_ones(blk)
    qs = [q_ref[:, g * d:(g + 1) * d].astype(BF16) for g in range(gh)]
    c_ref[...] = jnp.zeros_like(c_ref)
    acc_ref[...] = jnp.zeros_like(acc_ref)

    def step(j, mask):
        off = pl.multiple_of(j * blk, blk)
        top = None
        for g in range(gh):
            kj = k_ref[pl.ds(off, blk), g * d:(g + 1) * d].astype(BF16)
            vj = v_ref[pl.ds(off, blk), g * d:(g + 1) * d].astype(BF16)
            z = _dot_nt(qs[g], kj) * scale
            a, c = _sb_block(z, mask, c_ref[g], tri, ones)
            c_ref[g] = c
            acc_ref[g] += jnp.dot(a.astype(BF16), vj, preferred_element_type=F32)
            cmax = jnp.max(c)
            top = cmax if top is None else jnp.maximum(top, cmax)
        return top

    top = step(i, c_io < r)
    lax.while_loop(lambda jt: jnp.logical_and(jt[0] >= 0, jt[1] >= SB_EXIT),
                   lambda jt: (jt[0] - 1, step(jt[0], None)),
                   (i - 1, top))
    for g in range(gh):
        o_ref[:, g * d:(g + 1) * d] = acc_ref[g].astype(o_ref.dtype)


def _sb_prompt(q, k, v, B, T, H, out_dtype):
    blk = _tile(T, 256, LANES) if T % LANES == 0 else T
    nq = T // blk
    d = LANES
    gh = 2 if H % 2 == 0 else 1
    kern = functools.partial(_sb_prompt_kernel, scale=d ** -0.5, blk=blk, gh=gh)
    return pl.pallas_call(
        kern,
        grid=(B, H // gh, nq),
        in_specs=[
            pl.BlockSpec((blk, gh * d), lambda b, h, i: (b * nq + i, h)),
            pl.BlockSpec((T, gh * d), lambda b, h, i: (b, h)),
            pl.BlockSpec((T, gh * d), lambda b, h, i: (b, h)),
        ],
        out_specs=pl.BlockSpec((blk, gh * d), lambda b, h, i: (b * nq + i, h)),
        out_shape=jax.ShapeDtypeStruct((B * T, H * d), out_dtype),
        scratch_shapes=[pltpu.VMEM((gh, blk, LANES), F32), pltpu.VMEM((gh, blk, d), F32)],
        compiler_params=_cparams(("parallel", "parallel", "arbitrary")),
        name="sb_prompt",
    )(q, k, v)


def _pad_rows(x, rows):
    return jnp.pad(x, ((0, 0), (0, rows - x.shape[1]), (0, 0)))


def _sb_sample_kernel(pt_ref, q_ref, kn_ref, vn_ref, kc_hbm, vc_hbm, o_ref,
                      kbuf, vbuf, sem, c_ref, acc_ref, *, scale, dec, heads, n_pages):
    b = pl.program_id(0)
    d = LANES
    R = dec * heads
    tri, ones = _tri_ones(LANES)
    q = q_ref[0]
    qs = [q[:, h * d:(h + 1) * d].astype(BF16) for h in range(heads)]

    def copies(p, slot):
        page = pt_ref[b, n_pages - 1 - p]
        return (pltpu.make_async_copy(kc_hbm.at[page], kbuf.at[slot], sem.at[0, slot]),
                pltpu.make_async_copy(vc_hbm.at[page], vbuf.at[slot], sem.at[1, slot]))

    def start(p, slot):
        for cp in copies(p, slot):
            cp.start()

    def wait(p, slot):
        for cp in copies(p, slot):
            cp.wait()

    start(0, 0)

    def process(load_k, load_v, mask):
        z = jnp.concatenate([_dot_nt(qs[h], load_k(h)) for h in range(heads)], axis=0) * scale
        a, c = _sb_block(z, mask, c_ref[...], tri, ones)
        c_ref[...] = c
        pv = [jnp.dot(a[h * dec:(h + 1) * dec].astype(BF16), load_v(h), preferred_element_type=F32)
              for h in range(heads)]
        acc_ref[...] += jnp.concatenate(pv, axis=0)
        return jnp.max(c)

    c_ref[...] = jnp.zeros_like(c_ref)
    acc_ref[...] = jnp.zeros_like(acc_ref)
    rr, cc = _iota2(R, LANES)
    top = process(lambda h: kn_ref[0, :, h * d:(h + 1) * d].astype(BF16),
                  lambda h: vn_ref[0, :, h * d:(h + 1) * d].astype(BF16),
                  cc < jnp.bitwise_and(rr, dec - 1))

    def body(carry):
        p, _ = carry
        slot = jnp.bitwise_and(p, 1)
        wait(p, slot)

        @pl.when(p + 1 < n_pages)
        def _():
            start(p + 1, 1 - slot)

        top = process(lambda h: kbuf[slot, pl.ds(h, LANES, stride=heads), :].astype(BF16),
                      lambda h: vbuf[slot, pl.ds(h, LANES, stride=heads), :].astype(BF16),
                      None)
        return p + 1, top

    p_end, _ = lax.while_loop(lambda pt: jnp.logical_and(pt[0] < n_pages, pt[1] >= SB_EXIT),
                              body, (jnp.int32(0), top))

    @pl.when(p_end < n_pages)
    def _():
        wait(p_end, jnp.bitwise_and(p_end, 1))

    acc = acc_ref[...]
    for h in range(heads):
        o_ref[0, :, h * d:(h + 1) * d] = acc[h * dec:(h + 1) * dec]


def _sb_sample(q, k_new, v_new, cache_k, cache_v, page_table):
    Bs, Ts, W = q.shape
    n_pool, page, heads, d = cache_k.shape
    n_pages = page_table.shape[1]
    assert page == LANES and d == LANES and Ts & (Ts - 1) == 0
    R = Ts * heads
    kern = functools.partial(_sb_sample_kernel, scale=d ** -0.5, dec=Ts, heads=heads, n_pages=n_pages)
    per_b = lambda b, pt: (b, 0, 0)
    grid_spec = pltpu.PrefetchScalarGridSpec(
        num_scalar_prefetch=1,
        grid=(Bs,),
        in_specs=[
            pl.BlockSpec((1, Ts, W), per_b),
            pl.BlockSpec((1, page, W), per_b),
            pl.BlockSpec((1, page, W), per_b),
            pl.BlockSpec(memory_space=pl.ANY),
            pl.BlockSpec(memory_space=pl.ANY),
        ],
        out_specs=pl.BlockSpec((1, Ts, W), per_b),
        scratch_shapes=[
            pltpu.VMEM((2, page * heads, d), F32), pltpu.VMEM((2, page * heads, d), F32),
            pltpu.SemaphoreType.DMA((2, 2)),
            pltpu.VMEM((R, LANES), F32), pltpu.VMEM((R, d), F32),
        ],
    )
    return pl.pallas_call(
        kern,
        grid_spec=grid_spec,
        out_shape=jax.ShapeDtypeStruct((Bs, Ts, W), F32),
        compiler_params=_cparams(("arbitrary",)),
        name="sb_sample",
    )(page_table, q, _pad_rows(k_new, page), _pad_rows(v_new, page),
      cache_k.reshape(n_pool, page * heads, d), cache_v.reshape(n_pool, page * heads, d))


def _t5_bucket(rel, n_buckets):
    n = jnp.maximum(rel, 0)
    max_exact = n_buckets // 2
    nf = jnp.maximum(n, 1).astype(F32)
    large = max_exact + (jnp.log(nf / max_exact) / math.log(MAX_DISTANCE / max_exact)
                         * (n_buckets - max_exact)).astype(jnp.int32)
    large = jnp.minimum(large, n_buckets - 1)
    return jnp.where(n < max_exact, n, large)


def _bias_table(rel, rel_bias):
    nb = rel_bias.shape[0]
    onehot = jax.nn.one_hot(_t5_bucket(rel, nb), nb, dtype=F32)
    return jnp.einsum("...n,nh->h...", onehot, rel_bias.astype(F32), precision=lax.Precision.HIGHEST)


def _lambda(lv_ref):
    s1 = jnp.sum(lv_ref[0:1, :] * lv_ref[1:2, :], axis=-1, keepdims=True)
    s2 = jnp.sum(lv_ref[2:3, :] * lv_ref[3:4, :], axis=-1, keepdims=True)
    return jnp.exp(s1) - jnp.exp(s2) + LAMBDA_INIT


def _softmax_step(s, m, l, ones):
    m_new = jnp.maximum(m, jnp.max(s, axis=-1, keepdims=True))
    alpha = jnp.exp(m - m_new)
    p = jnp.exp(s - _rep(m_new, s.shape[1] // LANES)).astype(BF16)
    return p, alpha, m_new, alpha * l + jnp.dot(p, ones, preferred_element_type=F32)


def _sub_norm(x, g):
    ms = jnp.mean(x * x, axis=-1, keepdims=True)
    return x * lax.rsqrt(ms + EPS) * g * (1.0 - LAMBDA_INIT)


def _dc_prompt_kernel(q_ref, k_ref, v_ref, tab_ref, far_ref, lv_ref, g_ref, o_ref,
                      m_ref, l_ref, acc_ref, *, scale, blk):
    i = pl.program_id(2)
    d = LANES
    q = [q_ref[:, c * d:(c + 1) * d].astype(BF16) for c in range(2)]
    m_ref[...] = jnp.full_like(m_ref, NEG)
    l_ref[...] = jnp.zeros_like(l_ref)
    acc_ref[...] = jnp.zeros_like(acc_ref)

    def step(pieces):
        offs = [pl.multiple_of(j * blk, blk) for j, _, _ in pieces]
        vs = [v_ref[pl.ds(off, blk), :].astype(BF16) for off in offs]
        ones = jnp.ones((len(pieces) * blk, LANES), BF16)
        for c in range(2):
            parts = []
            for off, (_, bias, mask) in zip(offs, pieces):
                kj = k_ref[pl.ds(off, blk), c * d:(c + 1) * d].astype(BF16)
                sp = _dot_nt(q[c], kj) * scale + bias
                parts.append(sp if mask is None else jnp.where(mask, sp, NEG))
            s = parts[0] if len(parts) == 1 else jnp.concatenate(parts, axis=1)
            p, alpha, m_new, l_new = _softmax_step(s, m_ref[c], l_ref[c], ones)
            m_ref[c] = m_new
            l_ref[c] = l_new
            pv = None
            for t, vj in enumerate(vs):
                pt = jnp.dot(p[:, t * blk:(t + 1) * blk], vj, preferred_element_type=F32)
                pv = pt if pv is None else pv + pt
            acc_ref[c] = _rep(alpha, 2) * acc_ref[c] + pv

    nfar = jnp.maximum(i - 1, 0)
    far = far_ref[0]

    def far_body(t, carry):
        step([(2 * t, far, None), (2 * t + 1, far, None)])
        return carry

    lax.fori_loop(0, jnp.right_shift(nfar, 1), far_body, 0)

    @pl.when(jnp.bitwise_and(nfar, 1) == 1)
    def _():
        step([(nfar - 1, far, None)])

    r, c_io = _iota2(blk)
    causal = c_io <= r

    @pl.when(i >= 1)
    def _():
        step([(i - 1, tab_ref[0, 1], None), (i, tab_ref[0, 0], causal)])

    @pl.when(i == 0)
    def _():
        step([(i, tab_ref[0, 0], causal)])

    lam = _lambda(lv_ref)
    out = acc_ref[0] / _rep(l_ref[0], 2) - lam * (acc_ref[1] / _rep(l_ref[1], 2))
    o_ref[...] = _sub_norm(out, g_ref[...]).astype(o_ref.dtype)


def _dc_prompt(q, k, v, B, T, H, rel_bias, lam_vecs, sub_g, out_dtype):
    blk = _tile(T, 256, LANES) if T % LANES == 0 else T
    nq = T // blk
    d2 = 2 * LANES
    assert blk >= MAX_DISTANCE or nq == 1
    r = jnp.arange(blk, dtype=jnp.int32)
    rel0 = r[:, None] - r[None, :]
    tabs = jnp.stack([_bias_table(rel0, rel_bias), _bias_table(rel0 + blk, rel_bias)], axis=1)
    far = jnp.broadcast_to(rel_bias[-1].astype(F32)[:, None, None], (H, 1, blk))
    kern = functools.partial(_dc_prompt_kernel, scale=LANES ** -0.5, blk=blk)
    return pl.pallas_call(
        kern,
        grid=(B, H, nq),
        in_specs=[
            pl.BlockSpec((blk, d2), lambda b, h, i: (b * nq + i, h)),
            pl.BlockSpec((T, d2), lambda b, h, i: (b, h)),
            pl.BlockSpec((T, d2), lambda b, h, i: (b, h)),
            pl.BlockSpec((1, 2, blk, blk), lambda b, h, i: (h, 0, 0, 0)),
            pl.BlockSpec((1, 1, blk), lambda b, h, i: (h, 0, 0)),
            pl.BlockSpec((4, LANES), lambda b, h, i: (0, 0)),
            pl.BlockSpec((1, d2), lambda b, h, i: (0, 0)),
        ],
        out_specs=pl.BlockSpec((blk, d2), lambda b, h, i: (b * nq + i, h)),
        out_shape=jax.ShapeDtypeStruct((B * T, H * d2), out_dtype),
        scratch_shapes=[pltpu.VMEM((2, blk, LANES), F32), pltpu.VMEM((2, blk, LANES), F32),
                        pltpu.VMEM((2, blk, d2), F32)],
        compiler_params=_cparams(("parallel", "parallel", "arbitrary")),
        name="dc_prompt",
    )(q, k, v, tabs, far, lam_vecs, sub_g.reshape(1, d2))


def _dc_sample_kernel(pt_ref, q_ref, kn_ref, vn_ref, *rest, scale, dec, heads, n_pages, ppb):
    kc_refs = rest[:ppb]
    vc_refs = rest[ppb:2 * ppb]
    (bnew_ref, blast_ref, bfar_ref, lv_ref, g_ref, o_ref,
     m_ref, l_ref, acc_ref, spread_ref, own_ref) = rest[2 * ppb:]
    s = pl.program_id(1)
    ns = pl.num_programs(1)
    d = LANES
    d2 = 2 * LANES
    groups = 2 * heads
    R = dec * groups
    dshift = dec.bit_length() - 1
    hshift = heads.bit_length() - 1
    q = q_ref[0]
    qs = [q[:, g * d:(g + 1) * d].astype(BF16) for g in range(groups)]
    ones = jnp.ones((LANES, LANES), BF16)

    def process(load_k, pv_fn, bias, mask):
        sc = jnp.concatenate([_dot_nt(qs[g], load_k(g)) for g in range(groups)], axis=0) * scale + bias
        if mask is not None:
            sc = jnp.where(mask, sc, NEG)
        p, alpha, m_new, l_new = _softmax_step(sc, m_ref[...], l_ref[...], ones)
        m_ref[...] = m_new
        l_ref[...] = l_new
        acc_ref[...] = _rep(alpha, 2) * acc_ref[...] + pv_fn(p)

    @pl.when(s == 0)
    def _():
        m_ref[...] = jnp.full_like(m_ref, NEG)
        l_ref[...] = jnp.zeros_like(l_ref)
        acc_ref[...] = jnp.zeros_like(acc_ref)
        kk, jj = _iota2(LANES, LANES * heads)
        spread_ref[...] = jnp.where(jnp.right_shift(jj, hshift) == kk, 1.0, 0.0).astype(BF16)
        rr, jj = _iota2(R, LANES * heads)
        own_ref[...] = jnp.where(jnp.right_shift(rr, dshift + 1) == jnp.bitwise_and(jj, heads - 1),
                                 1.0, 0.0).astype(BF16)

        def pv_new(p):
            return jnp.concatenate(
                [jnp.dot(p[h * 2 * dec:(h + 1) * 2 * dec],
                         vn_ref[0, :, h * d2:(h + 1) * d2].astype(BF16), preferred_element_type=F32)
                 for h in range(heads)], axis=0)

        rr, cc = _iota2(R, LANES)
        process(lambda g: kn_ref[0, :, g * d:(g + 1) * d].astype(BF16), pv_new,
                bnew_ref[...], cc <= jnp.bitwise_and(rr, dec - 1))

    sc = jnp.concatenate(
        [jnp.concatenate([_dot_nt(qs[g], kc_refs[u][0, pl.ds(g, LANES, stride=groups), :].astype(BF16))
                          for g in range(groups)], axis=0) for u in range(ppb)], axis=1)
    bias = jnp.concatenate([jnp.where(s * ppb + u == n_pages - 1, blast_ref[...], bfar_ref[...])
                            for u in range(ppb)], axis=1)
    p, alpha, m_new, l_new = _softmax_step(sc * scale + bias, m_ref[...], l_ref[...],
                                           jnp.ones((ppb * LANES, LANES), BF16))
    m_ref[...] = m_new
    l_ref[...] = l_new
    pv = None
    for u in range(ppb):
        pe = jnp.dot(p[:, u * LANES:(u + 1) * LANES], spread_ref[...], preferred_element_type=F32)
        pu = jnp.dot(pe.astype(BF16) * own_ref[...], vc_refs[u][0].astype(BF16), preferred_element_type=F32)
        pv = pu if pv is None else pv + pu
    acc_ref[...] = _rep(alpha, 2) * acc_ref[...] + pv

    @pl.when(s == ns - 1)
    def _():
        lam = _lambda(lv_ref)
        on = acc_ref[...] / _rep(l_ref[...], 2)
        for h in range(heads):
            r0 = h * 2 * dec
            out = on[r0:r0 + dec] - lam * on[r0 + dec:r0 + 2 * dec]
            o_ref[0, :, h * d2:(h + 1) * d2] = _sub_norm(out, g_ref[...])


def _dc_sample(q, k_new, v_new, cache_k, cache_v, page_table, rel_bias, lam_vecs, sub_g):
    Bs, Ts, W = q.shape
    n_pool, page, heads, d2 = cache_v.shape
    n_pages = page_table.shape[1]
    past = n_pages * page
    assert page == LANES and page >= MAX_DISTANCE and d2 == 2 * LANES
    assert Ts & (Ts - 1) == 0 and heads & (heads - 1) == 0
    ppb = 4 if n_pages % 4 == 0 else 1
    R = Ts * 2 * heads
    tok = jnp.arange(Ts, dtype=jnp.int32)[:, None]
    col = jnp.arange(page, dtype=jnp.int32)[None, :]
    rows = lambda t: jnp.broadcast_to(t[:, None], (heads, 2, Ts, page)).reshape(R, page)
    b_new = rows(_bias_table(tok - col, rel_bias))
    b_last = rows(_bias_table(past + tok - ((n_pages - 1) * page + col), rel_bias))
    b_far = rows(jnp.broadcast_to(rel_bias[-1].astype(F32)[:, None, None], (heads, Ts, page)))
    kern = functools.partial(_dc_sample_kernel, scale=LANES ** -0.5, dec=Ts, heads=heads,
                             n_pages=n_pages, ppb=ppb)
    per_b = lambda b, s, pt: (b, 0, 0)
    const = lambda b, s, pt: (0, 0)
    page_spec = lambda rows_, cols_, u: pl.BlockSpec(
        (1, rows_, cols_), lambda b, s, pt: (pt[b, s * ppb + u], 0, 0))
    grid_spec = pltpu.PrefetchScalarGridSpec(
        num_scalar_prefetch=1,
        grid=(Bs, n_pages // ppb),
        in_specs=[
            pl.BlockSpec((1, Ts, W), per_b),
            pl.BlockSpec((1, page, W), per_b),
            pl.BlockSpec((1, page, W), per_b),
            *[page_spec(page * heads * 2, LANES, u) for u in range(ppb)],
            *[page_spec(page * heads, d2, u) for u in range(ppb)],
            pl.BlockSpec((R, page), const),
            pl.BlockSpec((R, page), const),
            pl.BlockSpec((R, page), const),
            pl.BlockSpec((4, LANES), const),
            pl.BlockSpec((1, d2), const),
        ],
        out_specs=pl.BlockSpec((1, Ts, W), per_b),
        scratch_shapes=[pltpu.VMEM((R, LANES), F32), pltpu.VMEM((R, LANES), F32), pltpu.VMEM((R, d2), F32),
                        pltpu.VMEM((page, page * heads), BF16), pltpu.VMEM((R, page * heads), BF16)],
    )
    ck = cache_k.reshape(n_pool, page * heads * 2, LANES)
    cv = cache_v.reshape(n_pool, page * heads, d2)
    return pl.pallas_call(
        kern,
        grid_spec=grid_spec,
        out_shape=jax.ShapeDtypeStruct((Bs, Ts, W), F32),
        compiler_params=_cparams(("parallel", "arbitrary")),
        name="dc_sample",
    )(page_table, q, _pad_rows(k_new, page), _pad_rows(v_new, page), *([ck] * ppb), *([cv] * ppb),
      b_new, b_last, b_far, lam_vecs, sub_g.reshape(1, d2))


def _layernorm(x, g, b):
    mu = jnp.mean(x, axis=-1, keepdims=True)
    xc = x - mu
    var = jnp.mean(xc * xc, axis=-1, keepdims=True)
    return xc * lax.rsqrt(var + EPS) * g + b


def _gmlp_prompt_kernel(u_ref, v_ref, lg_ref, lb_ref, ws_ref, bst_ref, d_ref, gv_ref, *, groups):
    vv = _layernorm(v_ref[...], lg_ref[...], lb_ref[...])
    gv_ref[0] = vv
    L = vv.shape[0]
    r, c = _iota2(L)
    tril = c <= r
    for g in range(groups):
        sl = slice(g * LANES, (g + 1) * LANES)
        w = jnp.where(tril, ws_ref[g], 0.0).astype(BF16)
        mixed = jnp.dot(w, vv[:, sl].astype(BF16), preferred_element_type=F32) + bst_ref[:, g:g + 1]
        d_ref[:, sl] = (u_ref[:, sl] * mixed).astype(d_ref.dtype)


def _gmlp_prompt(z, B, T, C, ln_g, ln_b, ws, bs, out_dtype):
    L = LANES
    nc = T // L
    groups = C // LANES
    kern = functools.partial(_gmlp_prompt_kernel, groups=groups)
    row = lambda b, c: (b * nc + c, 0)
    return pl.pallas_call(
        kern,
        grid=(B, nc),
        in_specs=[
            pl.BlockSpec((L, C), row),
            pl.BlockSpec((L, C), lambda b, c: (b * nc + c, 1)),
            pl.BlockSpec((1, C), lambda b, c: (0, 0)),
            pl.BlockSpec((1, C), lambda b, c: (0, 0)),
            pl.BlockSpec((groups, L, L), lambda b, c: (0, 0, 0)),
            pl.BlockSpec((L, groups), lambda b, c: (0, 0)),
        ],
        out_specs=[
            pl.BlockSpec((L, C), row),
            pl.BlockSpec((1, L, C), lambda b, c: (b, 0, 0)),
        ],
        out_shape=[
            jax.ShapeDtypeStruct((B * T, C), out_dtype),
            jax.ShapeDtypeStruct((B, L, C), F32),
        ],
        compiler_params=_cparams(("parallel", "arbitrary")),
        name="gmlp_prompt",
    )(z, z, ln_g.reshape(1, C), ln_b.reshape(1, C), ws[:, :L, :L], bs[:, :L].T)


def _gmlp_sample_kernel(u_ref, v_ref, lg_ref, lb_ref, wx_ref, bx_ref, d_ref, gv_ref):
    vv = _layernorm(v_ref[...], lg_ref[...], lb_ref[...])
    gv_ref[0] = vv
    L = vv.shape[0]
    mixed = bx_ref[...]
    for s in range(L):
        ws = wx_ref[s].astype(BF16).astype(F32)
        mixed = mixed + ws * vv[s:s + 1, :].astype(BF16).astype(F32)
    d_ref[...] = u_ref[...] * mixed


def _gmlp_sample(z, Bs, Ts, C, ln_g, ln_b, ws, bs):
    L = Ts
    w = jnp.tril(ws[:, :L, :L])
    wx = jnp.repeat(jnp.transpose(w, (2, 1, 0)), LANES, axis=2)
    bx = jnp.repeat(bs[:, :L].T, LANES, axis=1)
    row = lambda b: (b, 0)
    return pl.pallas_call(
        _gmlp_sample_kernel,
        grid=(Bs,),
        in_specs=[
            pl.BlockSpec((L, C), row),
            pl.BlockSpec((L, C), lambda b: (b, 1)),
            pl.BlockSpec((1, C), lambda b: (0, 0)),
            pl.BlockSpec((1, C), lambda b: (0, 0)),
            pl.BlockSpec((L, L, C), lambda b: (0, 0, 0)),
            pl.BlockSpec((L, C), lambda b: (0, 0)),
        ],
        out_specs=[
            pl.BlockSpec((L, C), row),
            pl.BlockSpec((1, L, C), lambda b: (b, 0, 0)),
        ],
        out_shape=[
            jax.ShapeDtypeStruct((Bs * Ts, C), F32),
            jax.ShapeDtypeStruct((Bs, L, C), F32),
        ],
        compiler_params=_cparams(("parallel",)),
        name="gmlp_sample",
    )(z, z, ln_g.reshape(1, C), ln_b.reshape(1, C), wx, bx)


def _run_group(x, mods, P, sample, wb):
    B, T, D = x.shape
    M = B * T
    C = P["conv_w"].shape[1]
    sbw = P["sb_w"]
    dcw = P["dc_w"]
    gc = P["gmlp_ln_g"].shape[0]
    act = F32 if sample is not None else BF16
    big = sample is None

    def gate_args(g):
        if big:
            return dict(gate=g.reshape(B, 1, D), rows_per_gate=T)
        return dict(gate=jnp.repeat(g, T, axis=0))

    def mm(name, a, w_key, *, layer=0, col_off=0, n_out=None, **kw):
        if big:
            out, wb[name] = _matmul(a, P[w_key], layer=layer, col_off=col_off, n_out=n_out,
                                    emit_w=True, **kw)
            return out
        return _matmul(a, wb[name], n_out=n_out, **kw)

    outs = {}
    for layer in range(2):
        sh1, sc1, gt1, sh2, sc2, gt2 = jnp.split(mods[layer], 6, axis=-1)
        h = _norm_mod(x, P["norm_g"][layer, 0], sc1, sh1, act).reshape(M, D)
        x2 = x.reshape(M, D)
        if layer == 0:
            ag = mm("even_ag", h, "w_in_even", n_out=2 * C)
            q = mm("even_q", h, "w_in_even", col_off=2 * C, n_out=sbw, out_dtype=act)
            k = mm("even_k", h, "w_in_even", col_off=2 * C + sbw, n_out=sbw)
            v = mm("even_v", h, "w_in_even", col_off=2 * C + 2 * sbw, n_out=sbw)
            heads = sbw // LANES
            if big:
                a_out, conv_st = _conv_module(ag, B, T, C, None, P["conv_w"], P["conv_b"],
                                              P["conv_ln_g"], P["conv_ln_b"], BF16)
                sb = _sb_prompt(q, k, v, B, T, heads, BF16)
            else:
                a_out, conv_st = _conv_module(ag, B, T, C, sample["state_conv"], P["conv_w"],
                                              P["conv_b"], P["conv_ln_g"], P["conv_ln_b"], F32)
                sb = _sb_sample(q.reshape(B, T, sbw), k.reshape(B, T, sbw), v.reshape(B, T, sbw),
                                sample["cache_sb_k"], sample["cache_sb_v"],
                                sample["page_table"]).reshape(M, sbw)
            outs["conv_st"] = conv_st
            outs["sb_k"] = k.reshape(B, T, heads, LANES)
            outs["sb_v"] = v.reshape(B, T, heads, LANES)
            mix = jnp.concatenate([a_out, sb], axis=-1)
            x2 = mm("even_out", mix, "w_out_even", n_out=D, res=x2, **gate_args(gt1))
        else:
            heads = dcw // (2 * LANES)
            qg = jnp.tile(P["q_norm_g"], dcw // LANES)
            kg = jnp.tile(P["k_norm_g"], dcw // LANES)
            q = mm("odd_q", h, "w_in_odd", n_out=dcw, epilogue="group_rmsnorm", vec=qg, out_dtype=act)
            k = mm("odd_k", h, "w_in_odd", col_off=dcw, n_out=dcw, epilogue="group_rmsnorm", vec=kg)
            v = mm("odd_v", h, "w_in_odd", col_off=2 * dcw, n_out=dcw)
            z = mm("odd_z", h, "w_in_odd", col_off=3 * dcw, n_out=2 * gc, epilogue="gelu")
            if big:
                c_out = _dc_prompt(q, k, v, B, T, heads, P["rel_bias"], P["lam_vecs"],
                                   P["sub_norm_g"], BF16)
                d_out, g_v = _gmlp_prompt(z, B, T, gc, P["gmlp_ln_g"], P["gmlp_ln_b"],
                                          P["gmlp_ws"], P["gmlp_bs"], BF16)
            else:
                c_out = _dc_sample(q.reshape(B, T, dcw), k.reshape(B, T, dcw), v.reshape(B, T, dcw),
                                   sample["cache_dc_k"], sample["cache_dc_v"], sample["page_table"],
                                   P["rel_bias"], P["lam_vecs"], P["sub_norm_g"]).reshape(M, dcw)
                d_out, g_v = _gmlp_sample(z, B, T, gc, P["gmlp_ln_g"], P["gmlp_ln_b"],
                                          P["gmlp_ws"], P["gmlp_bs"])
            outs["dc_k"] = k.reshape(B, T, heads, 2, LANES)
            outs["dc_v"] = v.reshape(B, T, heads, 2 * LANES)
            outs["g_v"] = g_v
            mix = jnp.concatenate([c_out, d_out], axis=-1)
            x2 = mm("odd_out", mix, "w_out_odd", n_out=D, res=x2, **gate_args(gt1))
        h = _norm_mod(x2.reshape(B, T, D), P["norm_g"][layer, 1], sc2, sh2, act).reshape(M, D)
        a = mm("mlp_in%d" % layer, h, "w_mlp_in", layer=layer, n_out=P["w_mlp_in"].shape[2],
               epilogue="relu2", out_dtype=BF16)
        x2 = _matmul(a, P["w_mlp_out"], layer=layer, res=x2, **gate_args(gt2))
        x = x2.reshape(B, T, D)
    return x, outs


def kernel(x_prompt, x_sample, state_conv, cache_sb_k, cache_sb_v, cache_dc_k, cache_dc_v, page_table, c_prompt, c_sample, norm_g, w_ada, b_ada, w_in_even, w_out_even, conv_w, conv_b, conv_ln_g, conv_ln_b, w_in_odd, w_out_odd, q_norm_g, k_norm_g, lambda_q1, lambda_k1, lambda_q2, lambda_k2, sub_norm_g, rel_bias, gmlp_ln_g, gmlp_ln_b, gmlp_ws, gmlp_bs, w_mlp_in, w_mlp_out):
    B, T, D = x_prompt.shape
    Bs, Ts, _ = x_sample.shape
    sbw = cache_sb_k.shape[2] * cache_sb_k.shape[3]
    dcw = cache_dc_v.shape[2] * cache_dc_v.shape[3]

    P = dict(
        norm_g=norm_g, conv_w=conv_w, conv_b=conv_b, conv_ln_g=conv_ln_g, conv_ln_b=conv_ln_b,
        q_norm_g=q_norm_g, k_norm_g=k_norm_g, sub_norm_g=sub_norm_g, rel_bias=rel_bias,
        gmlp_ln_g=gmlp_ln_g, gmlp_ln_b=gmlp_ln_b, gmlp_ws=gmlp_ws, gmlp_bs=gmlp_bs,
        lam_vecs=jnp.stack([lambda_q1, lambda_k1, lambda_q2, lambda_k2]).astype(F32),
        w_in_even=w_in_even[None], w_out_even=w_out_even[None],
        w_in_odd=w_in_odd[None], w_out_odd=w_out_odd[None],
        w_mlp_in=w_mlp_in, w_mlp_out=w_mlp_out.astype(BF16),
        sb_w=sbw, dc_w=dcw,
    )

    nc = B + Bs
    rows = -(-nc // 16) * 16
    c_all = jnp.pad(jnp.concatenate([c_prompt, c_sample], axis=0), ((0, rows - nc), (0, 0)))
    mods = [_matmul(c_all, w_ada, layer=l, a_silu=True, epilogue="bias", vec=b_ada[l], tn_pref=512)
            for l in range(w_ada.shape[0])]
    mods_p = [m[:B] for m in mods]
    mods_s = [m[B:nc] for m in mods]

    wb = {}
    y_p, o_p = _run_group(x_prompt, mods_p, P, None, wb)
    sample = dict(state_conv=state_conv, page_table=page_table, cache_sb_k=cache_sb_k,
                  cache_sb_v=cache_sb_v, cache_dc_k=cache_dc_k, cache_dc_v=cache_dc_v)
    y_s, o_s = _run_group(x_sample, mods_s, P, sample, wb)
    return (y_p, y_s, o_p["conv_st"], o_s["conv_st"],
            o_p["sb_k"], o_p["sb_v"], o_s["sb_k"], o_s["sb_v"],
            o_p["dc_k"], o_p["dc_v"], o_s["dc_k"], o_s["dc_v"],
            o_p["g_v"], o_s["g_v"])
```

```python
import functools
import math

import jax
import jax.numpy as jnp
from jax import lax
from jax.experimental import pallas as pl
from jax.experimental.pallas import tpu as pltpu

F32 = jnp.float32
BF16 = jnp.bfloat16

EPS = 1e-6
LAMBDA_INIT = 0.8 - 0.6 * math.exp(-0.3 * 1)
MAX_DISTANCE = 128
CONV_PAD = 32
LANES = 128
NEG = -1e30
SB_EXIT = -104.0

VMEM_LIMIT = 56 * 1024 * 1024


def _cparams(sem):
    return pltpu.CompilerParams(dimension_semantics=sem, vmem_limit_bytes=VMEM_LIMIT)


def _tile(dim, pref, align):
    if dim <= pref:
        return dim
    t = (pref // align) * align
    while t >= align:
        if dim % t == 0:
            return t
        t -= align
    return dim


def _norm_mod_kernel(x_ref, g_ref, sc_ref, sh_ref, o_ref):
    x = x_ref[0]
    ms = jnp.mean(x * x, axis=-1, keepdims=True)
    y = x * lax.rsqrt(ms + EPS) * g_ref[...]
    o_ref[0] = (y * (1.0 + sc_ref[0]) + sh_ref[0]).astype(o_ref.dtype)


def _norm_mod(x, g, sc, sh, out_dtype):
    B, T, D = x.shape
    tr = _tile(T, 256, 16)
    return pl.pallas_call(
        _norm_mod_kernel,
        grid=(B, T // tr),
        in_specs=[
            pl.BlockSpec((1, tr, D), lambda b, t: (b, t, 0)),
            pl.BlockSpec((1, D), lambda b, t: (0, 0)),
            pl.BlockSpec((1, 1, D), lambda b, t: (b, 0, 0)),
            pl.BlockSpec((1, 1, D), lambda b, t: (b, 0, 0)),
        ],
        out_specs=pl.BlockSpec((1, tr, D), lambda b, t: (b, t, 0)),
        out_shape=jax.ShapeDtypeStruct((B, T, D), out_dtype),
        compiler_params=_cparams(("parallel", "parallel")),
        name="norm_mod",
    )(x, g.reshape(1, D), sc.reshape(B, 1, D), sh.reshape(B, 1, D))


def _gelu_tanh(x):
    c = math.sqrt(2.0 / math.pi)
    return x * (0.5 * (1.0 + jnp.tanh(c * (x + 0.044715 * (x * x * x)))))


def _mm_kernel(*refs, nk, epilogue, a_silu, has_vec, has_res):
    it = iter(refs)
    a_ref = next(it)
    w_ref = next(it)
    vec_ref = next(it) if has_vec else None
    res_ref = next(it) if has_res else None
    gate_ref = next(it) if has_res else None
    o_ref = next(it)
    acc_ref = next(it) if nk > 1 else None

    w = w_ref[...].astype(BF16)
    a = a_ref[...]
    if a_silu:
        a = a * jax.nn.sigmoid(a)
    part = jnp.dot(a.astype(BF16), w, preferred_element_type=F32)

    def finish(acc):
        if epilogue == "relu2":
            r = jnp.maximum(acc, 0.0)
            out = r * r
        elif epilogue == "gelu":
            out = _gelu_tanh(acc)
        elif epilogue == "bias":
            out = acc + vec_ref[...]
        elif epilogue == "group_rmsnorm":
            tn = acc.shape[1]
            for g in range(tn // LANES):
                sl = slice(g * LANES, (g + 1) * LANES)
                x = acc[:, sl]
                ms = jnp.mean(x * x, axis=-1, keepdims=True)
                o_ref[:, sl] = (x * lax.rsqrt(ms + EPS) * vec_ref[:, sl]).astype(o_ref.dtype)
            return
        else:
            out = acc
        if has_res:
            gate = gate_ref[0] if len(gate_ref.shape) == 3 else gate_ref[...]
            out = res_ref[...] + gate * out
        o_ref[...] = out.astype(o_ref.dtype)

    if nk == 1:
        finish(part)
    else:
        k = pl.program_id(2)

        @pl.when(k == 0)
        def _():
            acc_ref[...] = part

        @pl.when(k > 0)
        def _():
            acc_ref[...] += part

        @pl.when(k == nk - 1)
        def _():
            finish(acc_ref[...])


def _matmul(a, w, *, layer=0, col_off=0, n_out=None, epilogue="none", a_silu=False,
            vec=None, res=None, gate=None, rows_per_gate=None, out_dtype=F32, tn_pref=1024):
    M, K = a.shape
    n_out = w.shape[2] - col_off if n_out is None else n_out
    tm = _tile(M if rows_per_gate is None else rows_per_gate, 1024, 16)
    tk = _tile(K, 4096, LANES)
    nk = K // tk
    if res is not None:
        tn_pref = min(tn_pref, 512)
    tn = _tile(math.gcd(n_out, col_off), tn_pref, LANES)
    assert col_off % tn == 0
    joff = col_off // tn
    grid = (M // tm, n_out // tn, nk)
    spec = pl.BlockSpec

    in_specs = [
        spec((tm, tk), lambda i, j, k: (i, k)),
        spec((None, tk, tn), lambda i, j, k: (layer, k, j + joff)),
    ]
    args = [a, w]
    if vec is not None:
        in_specs.append(spec((1, tn), lambda i, j, k: (0, j)))
        args.append(vec.reshape(1, n_out))
    if res is not None:
        in_specs.append(spec((tm, tn), lambda i, j, k: (i, j)))
        args.append(res)
        if gate.ndim == 3:
            assert rows_per_gate % tm == 0
            rpg = rows_per_gate // tm
            in_specs.append(spec((1, 1, tn), lambda i, j, k: (i // rpg, 0, j)))
        else:
            in_specs.append(spec((tm, tn), lambda i, j, k: (i, j)))
        args.append(gate)
    scratch = [pltpu.VMEM((tm, tn), F32)] if nk > 1 else []
    kern = functools.partial(_mm_kernel, nk=nk, epilogue=epilogue, a_silu=a_silu,
                             has_vec=vec is not None, has_res=res is not None)
    return pl.pallas_call(
        kern,
        grid=grid,
        in_specs=in_specs,
        out_specs=spec((tm, tn), lambda i, j, k: (i, j)),
        out_shape=jax.ShapeDtypeStruct((M, n_out), out_dtype),
        scratch_shapes=scratch,
        compiler_params=_cparams(("parallel", "parallel", "arbitrary")),
        name="mm_" + epilogue,
    )(*args)


def _conv_kernel(val_ref, gate_ref, pre_ref, w_ref, b_ref, lg_ref, lb_ref, o_ref, st_ref,
                 xx_ref, cv_ref, *, tr, width, rc, cc):
    t = pl.program_id(1)
    nt = pl.num_programs(1)
    C = val_ref.shape[1]

    @pl.when(t == 0)
    def _():
        xx_ref[0:CONV_PAD, :] = pre_ref[0]

    xx_ref[CONV_PAD:CONV_PAD + tr, :] = val_ref[...] * jax.nn.sigmoid(gate_ref[...])

    base = CONV_PAD - (width - 1)
    sub = 8
    for r0 in range(0, tr, rc):
        for c0 in range(0, C, cc):
            acc = jnp.zeros((rc, cc), F32)
            for s in range(sub):
                rows = rc + (sub if s else 0)
                part = None
                for u in range(s, base + width, sub):
                    if u < base:
                        continue
                    term = (w_ref[u - base:u - base + 1, c0:c0 + cc]
                            * xx_ref[r0 + u - s:r0 + u - s + rows, c0:c0 + cc])
                    part = term if part is None else part + term
                acc = acc + part[s:s + rc]
            cv_ref[r0:r0 + rc, c0:c0 + cc] = acc + b_ref[:, c0:c0 + cc]

    cv = cv_ref[...]
    mu = jnp.mean(cv, axis=-1, keepdims=True)
    xc = cv - mu
    var = jnp.mean(xc * xc, axis=-1, keepdims=True)
    y = xc * lax.rsqrt(var + EPS) * lg_ref[...] + lb_ref[...]
    o_ref[...] = (y * jax.nn.sigmoid(y)).astype(o_ref.dtype)

    @pl.when(t == nt - 1)
    def _():
        st_ref[0] = xx_ref[tr:tr + CONV_PAD, :]

    @pl.when(t < nt - 1)
    def _():
        xx_ref[0:CONV_PAD, :] = xx_ref[tr:tr + CONV_PAD, :]


def _conv_module(proj, B, T, C, prefix, conv_w, conv_b, ln_g, ln_b, out_dtype):
    width = conv_w.shape[0]
    hist = width - 1
    assert hist <= CONV_PAD
    tr = _tile(T, 256, 8)
    nt = T // tr
    assert nt == 1 or tr >= CONV_PAD
    if prefix is None:
        pre = jnp.zeros((B, CONV_PAD, C), F32)
    else:
        pre = jnp.pad(prefix.astype(F32), ((0, 0), (CONV_PAD - hist, 0), (0, 0)))
    rc = _tile(tr, 32, 8)
    cc = _tile(C, 256, LANES)
    kern = functools.partial(_conv_kernel, tr=tr, width=width, rc=rc, cc=cc)
    row = lambda b, t: (b * nt + t, 0)
    a_out, st = pl.pallas_call(
        kern,
        grid=(B, nt),
        in_specs=[
            pl.BlockSpec((tr, C), row),
            pl.BlockSpec((tr, C), lambda b, t: (b * nt + t, 1)),
            pl.BlockSpec((1, CONV_PAD, C), lambda b, t: (b, 0, 0)),
            pl.BlockSpec((width, C), lambda b, t: (0, 0)),
            pl.BlockSpec((1, C), lambda b, t: (0, 0)),
            pl.BlockSpec((1, C), lambda b, t: (0, 0)),
            pl.BlockSpec((1, C), lambda b, t: (0, 0)),
        ],
        out_specs=[
            pl.BlockSpec((tr, C), row),
            pl.BlockSpec((1, CONV_PAD, C), lambda b, t: (b, 0, 0)),
        ],
        out_shape=[
            jax.ShapeDtypeStruct((B * T, C), out_dtype),
            jax.ShapeDtypeStruct((B, CONV_PAD, C), F32),
        ],
        scratch_shapes=[pltpu.VMEM((CONV_PAD + tr, C), F32), pltpu.VMEM((tr, C), F32)],
        compiler_params=_cparams(("parallel", "arbitrary")),
        name="conv_module",
    )(proj, proj, pre, conv_w, conv_b.reshape(1, C), ln_g.reshape(1, C), ln_b.reshape(1, C))
    return a_out, st[:, CONV_PAD - hist:, :]


def _dot_nt(a, b):
    return lax.dot_general(a, b, (((1,), (1,)), ((), ())), preferred_element_type=F32)


def _iota2(rows, cols=None):
    shape = (rows, rows if cols is None else cols)
    return lax.broadcasted_iota(jnp.int32, shape, 0), lax.broadcasted_iota(jnp.int32, shape, 1)


def _rep(x, n):
    return x if n == 1 else jnp.tile(x, (1, n))


def _tri_ones(kb):
    r, c = _iota2(kb)
    return jnp.where(r > c, 1.0, 0.0).astype(BF16), jnp.ones((kb, LANES), BF16)


def _sb_block(z, mask, c, tri, ones):
    sp = jnp.maximum(z, 0.0) + jnp.log1p(jnp.exp(-jnp.abs(z)))
    ln = -sp if mask is None else jnp.where(mask, -sp, 0.0)
    hi = ln.astype(BF16)
    lo = (ln - hi.astype(F32)).astype(BF16)
    within = (jnp.dot(hi, tri, preferred_element_type=F32)
              + jnp.dot(lo, tri, preferred_element_type=F32))
    total = (jnp.dot(hi, ones, preferred_element_type=F32)
             + jnp.dot(lo, ones, preferred_element_type=F32))
    a = jnp.exp(z - sp + within + _rep(c, z.shape[1] // LANES))
    if mask is not None:
        a = jnp.where(mask, a, 0.0)
    return a, c + total


def _sb_prompt_kernel(q_ref, k_ref, v_ref, o_ref, c_ref, acc_ref, *, scale, blk, gh):
    i = pl.program_id(2)
    d = LANES
    r, c_io = _iota2(blk)
    tri, ones = _tri_ones(blk)
    qs = [q_ref[:, g * d:(g + 1) * d].astype(BF16) for g in range(gh)]
    c_ref[...] = jnp.zeros_like(c_ref)
    acc_ref[...] = jnp.zeros_like(acc_ref)

    def step(j, mask):
        off = pl.multiple_of(j * blk, blk)
        top = None
        for g in range(gh):
            kj = k_ref[pl.ds(off, blk), g * d:(g + 1) * d].astype(BF16)
            vj = v_ref[pl.ds(off, blk), g * d:(g + 1) * d].astype(BF16)
            z = _dot_nt(qs[g], kj) * scale
            a, c = _sb_block(z, mask, c_ref[g], tri, ones)
            c_ref[g] = c
            acc_ref[g] += jnp.dot(a.astype(BF16), vj, preferred_element_type=F32)
            cmax = jnp.max(c)
            top = cmax if top is None else jnp.maximum(top, cmax)
        return top

    top = step(i, c_io < r)
    lax.while_loop(lambda jt: jnp.logical_and(jt[0] >= 0, jt[1] >= SB_EXIT),
                   lambda jt: (jt[0] - 1, step(jt[0], None)),
                   (i - 1, top))
    for g in range(gh):
        o_ref[:, g * d:(g + 1) * d] = acc_ref[g].astype(o_ref.dtype)


def _sb_prompt(q, k, v, B, T, H, out_dtype):
    blk = _tile(T, 256, LANES) if T % LANES == 0 else T
    nq = T // blk
    d = LANES
    gh = next(g for g in (4, 2, 1) if H % g == 0)
    kern = functools.partial(_sb_prompt_kernel, scale=d ** -0.5, blk=blk, gh=gh)
    return pl.pallas_call(
        kern,
        grid=(B, H // gh, nq),
        in_specs=[
            pl.BlockSpec((blk, gh * d), lambda b, h, i: (b * nq + i, h)),
            pl.BlockSpec((T, gh * d), lambda b, h, i: (b, h)),
            pl.BlockSpec((T, gh * d), lambda b, h, i: (b, h)),
        ],
        out_specs=pl.BlockSpec((blk, gh * d), lambda b, h, i: (b * nq + i, h)),
        out_shape=jax.ShapeDtypeStruct((B * T, H * d), out_dtype),
        scratch_shapes=[pltpu.VMEM((gh, blk, LANES), F32), pltpu.VMEM((gh, blk, d), F32)],
        compiler_params=_cparams(("parallel", "parallel", "arbitrary")),
        name="sb_prompt",
    )(q, k, v)


def _pad_rows(x, rows):
    return jnp.pad(x, ((0, 0), (0, rows - x.shape[1]), (0, 0)))


def _sb_sample_kernel(pt_ref, q_ref, kn_ref, vn_ref, kc_hbm, vc_hbm, o_ref,
                      kbuf, vbuf, sem, c_ref, acc_ref, *, scale, dec, heads, n_pages):
    b = pl.program_id(0)
    d = LANES
    R = dec * heads
    tri, ones = _tri_ones(LANES)
    q = q_ref[0]
    qs = [q[:, h * d:(h + 1) * d].astype(BF16) for h in range(heads)]

    def copies(p, slot):
        page = pt_ref[b, n_pages - 1 - p]
        return (pltpu.make_async_copy(kc_hbm.at[page], kbuf.at[slot], sem.at[0, slot]),
                pltpu.make_async_copy(vc_hbm.at[page], vbuf.at[slot], sem.at[1, slot]))

    def start(p, slot):
        for cp in copies(p, slot):
            cp.start()

    def wait(p, slot):
        for cp in copies(p, slot):
            cp.wait()

    start(0, 0)

    def process(load_k, load_v, mask):
        z = jnp.concatenate([_dot_nt(qs[h], load_k(h)) for h in range(heads)], axis=0) * scale
        a, c = _sb_block(z, mask, c_ref[...], tri, ones)
        c_ref[...] = c
        pv = [jnp.dot(a[h * dec:(h + 1) * dec].astype(BF16), load_v(h), preferred_element_type=F32)
              for h in range(heads)]
        acc_ref[...] += jnp.concatenate(pv, axis=0)
        return jnp.max(c)

    c_ref[...] = jnp.zeros_like(c_ref)
    acc_ref[...] = jnp.zeros_like(acc_ref)
    rr, cc = _iota2(R, LANES)
    top = process(lambda h: kn_ref[0, :, h * d:(h + 1) * d].astype(BF16),
                  lambda h: vn_ref[0, :, h * d:(h + 1) * d].astype(BF16),
                  cc < jnp.bitwise_and(rr, dec - 1))

    def body(carry):
        p, _ = carry
        slot = jnp.bitwise_and(p, 1)
        wait(p, slot)

        @pl.when(p + 1 < n_pages)
        def _():
            start(p + 1, 1 - slot)

        top = process(lambda h: kbuf[slot, pl.ds(h, LANES, stride=heads), :].astype(BF16),
                      lambda h: vbuf[slot, pl.ds(h, LANES, stride=heads), :].astype(BF16),
                      None)
        return p + 1, top

    p_end, _ = lax.while_loop(lambda pt: jnp.logical_and(pt[0] < n_pages, pt[1] >= SB_EXIT),
                              body, (jnp.int32(0), top))

    @pl.when(p_end < n_pages)
    def _():
        wait(p_end, jnp.bitwise_and(p_end, 1))

    acc = acc_ref[...]
    for h in range(heads):
        o_ref[0, :, h * d:(h + 1) * d] = acc[h * dec:(h + 1) * dec]


def _sb_sample(q, k_new, v_new, cache_k, cache_v, page_table):
    Bs, Ts, W = q.shape
    n_pool, page, heads, d = cache_k.shape
    n_pages = page_table.shape[1]
    assert page == LANES and d == LANES and Ts & (Ts - 1) == 0
    R = Ts * heads
    kern = functools.partial(_sb_sample_kernel, scale=d ** -0.5, dec=Ts, heads=heads, n_pages=n_pages)
    per_b = lambda b, pt: (b, 0, 0)
    grid_spec = pltpu.PrefetchScalarGridSpec(
        num_scalar_prefetch=1,
        grid=(Bs,),
        in_specs=[
            pl.BlockSpec((1, Ts, W), per_b),
            pl.BlockSpec((1, page, W), per_b),
            pl.BlockSpec((1, page, W), per_b),
            pl.BlockSpec(memory_space=pl.ANY),
            pl.BlockSpec(memory_space=pl.ANY),
        ],
        out_specs=pl.BlockSpec((1, Ts, W), per_b),
        scratch_shapes=[
            pltpu.VMEM((2, page * heads, d), F32), pltpu.VMEM((2, page * heads, d), F32),
            pltpu.SemaphoreType.DMA((2, 2)),
            pltpu.VMEM((R, LANES), F32), pltpu.VMEM((R, d), F32),
        ],
    )
    return pl.pallas_call(
        kern,
        grid_spec=grid_spec,
        out_shape=jax.ShapeDtypeStruct((Bs, Ts, W), F32),
        compiler_params=_cparams(("arbitrary",)),
        name="sb_sample",
    )(page_table, q, _pad_rows(k_new, page), _pad_rows(v_new, page),
      cache_k.reshape(n_pool, page * heads, d), cache_v.reshape(n_pool, page * heads, d))


def _t5_bucket(rel, n_buckets):
    n = jnp.maximum(rel, 0)
    max_exact = n_buckets // 2
    nf = jnp.maximum(n, 1).astype(F32)
    large = max_exact + (jnp.log(nf / max_exact) / math.log(MAX_DISTANCE / max_exact)
                         * (n_buckets - max_exact)).astype(jnp.int32)
    large = jnp.minimum(large, n_buckets - 1)
    return jnp.where(n < max_exact, n, large)


def _bias_table(rel, rel_bias):
    nb = rel_bias.shape[0]
    onehot = jax.nn.one_hot(_t5_bucket(rel, nb), nb, dtype=F32)
    return jnp.einsum("...n,nh->h...", onehot, rel_bias.astype(F32), precision=lax.Precision.HIGHEST)


def _lambda(lv_ref):
    s1 = jnp.sum(lv_ref[0:1, :] * lv_ref[1:2, :], axis=-1, keepdims=True)
    s2 = jnp.sum(lv_ref[2:3, :] * lv_ref[3:4, :], axis=-1, keepdims=True)
    return jnp.exp(s1) - jnp.exp(s2) + LAMBDA_INIT


def _softmax_step(s, m, l, ones):
    m_new = jnp.maximum(m, jnp.max(s, axis=-1, keepdims=True))
    alpha = jnp.exp(m - m_new)
    p = jnp.exp(s - _rep(m_new, s.shape[1] // LANES)).astype(BF16)
    return p, alpha, m_new, alpha * l + jnp.dot(p, ones, preferred_element_type=F32)


def _sub_norm(x, g):
    ms = jnp.mean(x * x, axis=-1, keepdims=True)
    return x * lax.rsqrt(ms + EPS) * g * (1.0 - LAMBDA_INIT)


def _dc_prompt_kernel(q_ref, k_ref, v_ref, tab_ref, far_ref, lv_ref, g_ref, o_ref,
                      m_ref, l_ref, acc_ref, *, scale, blk, hp):
    i = pl.program_id(2)
    d = LANES
    d2 = 2 * LANES
    nch = 2 * hp
    q = [q_ref[:, n * d:(n + 1) * d].astype(BF16) for n in range(nch)]
    m_ref[...] = jnp.full_like(m_ref, NEG)
    l_ref[...] = jnp.zeros_like(l_ref)
    acc_ref[...] = jnp.zeros_like(acc_ref)

    def step(pieces):
        offs = [pl.multiple_of(j * blk, blk) for j, _, _ in pieces]
        ones = jnp.ones((len(pieces) * blk, LANES), BF16)
        for hh in range(hp):
            vs = [v_ref[pl.ds(off, blk), hh * d2:(hh + 1) * d2].astype(BF16) for off in offs]
            for c in range(2):
                n = 2 * hh + c
                parts = []
                for off, (_, bias_fn, mask) in zip(offs, pieces):
                    kj = k_ref[pl.ds(off, blk), n * d:(n + 1) * d].astype(BF16)
                    sp = _dot_nt(q[n], kj) * scale + bias_fn(hh)
                    parts.append(sp if mask is None else jnp.where(mask, sp, NEG))
                s = parts[0] if len(parts) == 1 else jnp.concatenate(parts, axis=1)
                p, alpha, m_new, l_new = _softmax_step(s, m_ref[n], l_ref[n], ones)
                m_ref[n] = m_new
                l_ref[n] = l_new
                pv = None
                for t, vj in enumerate(vs):
                    pt = jnp.dot(p[:, t * blk:(t + 1) * blk], vj, preferred_element_type=F32)
                    pv = pt if pv is None else pv + pt
                acc_ref[n] = _rep(alpha, 2) * acc_ref[n] + pv

    nfar = jnp.maximum(i - 1, 0)
    far = lambda hh: far_ref[hh]
    tab = lambda idx: (lambda hh: tab_ref[hh, idx])

    def far_body(t, carry):
        step([(2 * t, far, None), (2 * t + 1, far, None)])
        return carry

    lax.fori_loop(0, jnp.right_shift(nfar, 1), far_body, 0)

    @pl.when(jnp.bitwise_and(nfar, 1) == 1)
    def _():
        step([(nfar - 1, far, None)])

    r, c_io = _iota2(blk)
    causal = c_io <= r

    @pl.when(i >= 1)
    def _():
        step([(i - 1, tab(1), None), (i, tab(0), causal)])

    @pl.when(i == 0)
    def _():
        step([(i, tab(0), causal)])

    lam = _lambda(lv_ref)
    for hh in range(hp):
        out = (acc_ref[2 * hh] / _rep(l_ref[2 * hh], 2)
               - lam * (acc_ref[2 * hh + 1] / _rep(l_ref[2 * hh + 1], 2)))
        o_ref[:, hh * d2:(hh + 1) * d2] = _sub_norm(out, g_ref[...]).astype(o_ref.dtype)


def _dc_prompt(q, k, v, B, T, H, rel_bias, lam_vecs, sub_g, out_dtype):
    blk = _tile(T, 256, LANES) if T % LANES == 0 else T
    nq = T // blk
    d2 = 2 * LANES
    hp = 2 if H % 2 == 0 else 1
    assert blk >= MAX_DISTANCE or nq == 1
    r = jnp.arange(blk, dtype=jnp.int32)
    rel0 = r[:, None] - r[None, :]
    tabs = jnp.stack([_bias_table(rel0, rel_bias), _bias_table(rel0 + blk, rel_bias)], axis=1)
    far = jnp.broadcast_to(rel_bias[-1].astype(F32)[:, None, None], (H, 1, blk))
    kern = functools.partial(_dc_prompt_kernel, scale=LANES ** -0.5, blk=blk, hp=hp)
    return pl.pallas_call(
        kern,
        grid=(B, H // hp, nq),
        in_specs=[
            pl.BlockSpec((blk, hp * d2), lambda b, h, i: (b * nq + i, h)),
            pl.BlockSpec((T, hp * d2), lambda b, h, i: (b, h)),
            pl.BlockSpec((T, hp * d2), lambda b, h, i: (b, h)),
            pl.BlockSpec((hp, 2, blk, blk), lambda b, h, i: (h, 0, 0, 0)),
            pl.BlockSpec((hp, 1, blk), lambda b, h, i: (h, 0, 0)),
            pl.BlockSpec((4, LANES), lambda b, h, i: (0, 0)),
            pl.BlockSpec((1, d2), lambda b, h, i: (0, 0)),
        ],
        out_specs=pl.BlockSpec((blk, hp * d2), lambda b, h, i: (b * nq + i, h)),
        out_shape=jax.ShapeDtypeStruct((B * T, H * d2), out_dtype),
        scratch_shapes=[pltpu.VMEM((2 * hp, blk, LANES), F32), pltpu.VMEM((2 * hp, blk, LANES), F32),
                        pltpu.VMEM((2 * hp, blk, d2), F32)],
        compiler_params=_cparams(("parallel", "parallel", "arbitrary")),
        name="dc_prompt",
    )(q, k, v, tabs, far, lam_vecs, sub_g.reshape(1, d2))


def _dc_sample_kernel(pt_ref, q_ref, kn_ref, vn_ref, *rest, scale, dec, heads, n_pages, ppb):
    kc_refs = rest[:ppb]
    vc_refs = rest[ppb:2 * ppb]
    (bnew_ref, blast_ref, bfar_ref, lv_ref, g_ref, o_ref,
     m_ref, l_ref, acc_ref, spread_ref, own_ref) = rest[2 * ppb:]
    s = pl.program_id(1)
    ns = pl.num_programs(1)
    d = LANES
    d2 = 2 * LANES
    groups = 2 * heads
    R = dec * groups
    dshift = dec.bit_length() - 1
    hshift = heads.bit_length() - 1
    q = q_ref[0]
    qs = [q[:, g * d:(g + 1) * d].astype(BF16) for g in range(groups)]
    ones = jnp.ones((LANES, LANES), BF16)

    def process(load_k, pv_fn, bias, mask):
        sc = jnp.concatenate([_dot_nt(qs[g], load_k(g)) for g in range(groups)], axis=0) * scale + bias
        if mask is not None:
            sc = jnp.where(mask, sc, NEG)
        p, alpha, m_new, l_new = _softmax_step(sc, m_ref[...], l_ref[...], ones)
        m_ref[...] = m_new
        l_ref[...] = l_new
        acc_ref[...] = _rep(alpha, 2) * acc_ref[...] + pv_fn(p)

    @pl.when(s == 0)
    def _():
        m_ref[...] = jnp.full_like(m_ref, NEG)
        l_ref[...] = jnp.zeros_like(l_ref)
        acc_ref[...] = jnp.zeros_like(acc_ref)
        kk, jj = _iota2(LANES, LANES * heads)
        spread_ref[...] = jnp.where(jnp.right_shift(jj, hshift) == kk, 1.0, 0.0).astype(BF16)
        rr, jj = _iota2(R, LANES * heads)
        own_ref[...] = jnp.where(jnp.right_shift(rr, dshift + 1) == jnp.bitwise_and(jj, heads - 1),
                                 1.0, 0.0).astype(BF16)

        def pv_new(p):
            return jnp.concatenate(
                [jnp.dot(p[h * 2 * dec:(h + 1) * 2 * dec],
                         vn_ref[0, :, h * d2:(h + 1) * d2].astype(BF16), preferred_element_type=F32)
                 for h in range(heads)], axis=0)

        rr, cc = _iota2(R, LANES)
        process(lambda g: kn_ref[0, :, g * d:(g + 1) * d].astype(BF16), pv_new,
                bnew_ref[...], cc <= jnp.bitwise_and(rr, dec - 1))

    sc = jnp.concatenate(
        [jnp.concatenate([_dot_nt(qs[g], kc_refs[u][0, pl.ds(g, LANES, stride=groups), :].astype(BF16))
                          for g in range(groups)], axis=0) for u in range(ppb)], axis=1)
    bias = jnp.concatenate([jnp.where(s * ppb + u == n_pages - 1, blast_ref[...], bfar_ref[...])
                            for u in range(ppb)], axis=1)
    p, alpha, m_new, l_new = _softmax_step(sc * scale + bias, m_ref[...], l_ref[...],
                                           jnp.ones((ppb * LANES, LANES), BF16))
    m_ref[...] = m_new
    l_ref[...] = l_new
    pv = None
    for u in range(ppb):
        pe = jnp.dot(p[:, u * LANES:(u + 1) * LANES], spread_ref[...], preferred_element_type=F32)
        pu = jnp.dot(pe.astype(BF16) * own_ref[...], vc_refs[u][0].astype(BF16), preferred_element_type=F32)
        pv = pu if pv is None else pv + pu
    acc_ref[...] = _rep(alpha, 2) * acc_ref[...] + pv

    @pl.when(s == ns - 1)
    def _():
        lam = _lambda(lv_ref)
        on = acc_ref[...] / _rep(l_ref[...], 2)
        for h in range(heads):
            r0 = h * 2 * dec
            out = on[r0:r0 + dec] - lam * on[r0 + dec:r0 + 2 * dec]
            o_ref[0, :, h * d2:(h + 1) * d2] = _sub_norm(out, g_ref[...])


def _dc_sample(q, k_new, v_new, cache_k, cache_v, page_table, rel_bias, lam_vecs, sub_g):
    Bs, Ts, W = q.shape
    n_pool, page, heads, d2 = cache_v.shape
    n_pages = page_table.shape[1]
    past = n_pages * page
    assert page == LANES and page >= MAX_DISTANCE and d2 == 2 * LANES
    assert Ts & (Ts - 1) == 0 and heads & (heads - 1) == 0
    ppb = 4 if n_pages % 4 == 0 else 1
    R = Ts * 2 * heads
    tok = jnp.arange(Ts, dtype=jnp.int32)[:, None]
    col = jnp.arange(page, dtype=jnp.int32)[None, :]
    rows = lambda t: jnp.broadcast_to(t[:, None], (heads, 2, Ts, page)).reshape(R, page)
    b_new = rows(_bias_table(tok - col, rel_bias))
    b_last = rows(_bias_table(past + tok - ((n_pages - 1) * page + col), rel_bias))
    b_far = rows(jnp.broadcast_to(rel_bias[-1].astype(F32)[:, None, None], (heads, Ts, page)))
    kern = functools.partial(_dc_sample_kernel, scale=LANES ** -0.5, dec=Ts, heads=heads,
                             n_pages=n_pages, ppb=ppb)
    per_b = lambda b, s, pt: (b, 0, 0)
    const = lambda b, s, pt: (0, 0)
    page_spec = lambda rows_, cols_, u: pl.BlockSpec(
        (1, rows_, cols_), lambda b, s, pt: (pt[b, s * ppb + u], 0, 0))
    grid_spec = pltpu.PrefetchScalarGridSpec(
        num_scalar_prefetch=1,
        grid=(Bs, n_pages // ppb),
        in_specs=[
            pl.BlockSpec((1, Ts, W), per_b),
            pl.BlockSpec((1, page, W), per_b),
            pl.BlockSpec((1, page, W), per_b),
            *[page_spec(page * heads * 2, LANES, u) for u in range(ppb)],
            *[page_spec(page * heads, d2, u) for u in range(ppb)],
            pl.BlockSpec((R, page), const),
            pl.BlockSpec((R, page), const),
            pl.BlockSpec((R, page), const),
            pl.BlockSpec((4, LANES), const),
            pl.BlockSpec((1, d2), const),
        ],
        out_specs=pl.BlockSpec((1, Ts, W), per_b),
        scratch_shapes=[pltpu.VMEM((R, LANES), F32), pltpu.VMEM((R, LANES), F32), pltpu.VMEM((R, d2), F32),
                        pltpu.VMEM((page, page * heads), BF16), pltpu.VMEM((R, page * heads), BF16)],
    )
    ck = cache_k.reshape(n_pool, page * heads * 2, LANES)
    cv = cache_v.reshape(n_pool, page * heads, d2)
    return pl.pallas_call(
        kern,
        grid_spec=grid_spec,
        out_shape=jax.ShapeDtypeStruct((Bs, Ts, W), F32),
        compiler_params=_cparams(("parallel", "arbitrary")),
        name="dc_sample",
    )(page_table, q, _pad_rows(k_new, page), _pad_rows(v_new, page), *([ck] * ppb), *([cv] * ppb),
      b_new, b_last, b_far, lam_vecs, sub_g.reshape(1, d2))


def _layernorm(x, g, b):
    mu = jnp.mean(x, axis=-1, keepdims=True)
    xc = x - mu
    var = jnp.mean(xc * xc, axis=-1, keepdims=True)
    return xc * lax.rsqrt(var + EPS) * g + b


def _gmlp_prompt_kernel(u_ref, v_ref, lg_ref, lb_ref, ws_ref, bst_ref, d_ref, gv_ref, *, groups):
    vv = _layernorm(v_ref[...], lg_ref[...], lb_ref[...])
    gv_ref[0] = vv
    L = vv.shape[0]
    r, c = _iota2(L)
    tril = c <= r
    for g in range(groups):
        sl = slice(g * LANES, (g + 1) * LANES)
        w = jnp.where(tril, ws_ref[g], 0.0).astype(BF16)
        mixed = jnp.dot(w, vv[:, sl].astype(BF16), preferred_element_type=F32) + bst_ref[:, g:g + 1]
        d_ref[:, sl] = (u_ref[:, sl] * mixed).astype(d_ref.dtype)


def _gmlp_prompt(z, B, T, C, ln_g, ln_b, ws, bs, out_dtype):
    L = LANES
    nc = T // L
    groups = C // LANES
    kern = functools.partial(_gmlp_prompt_kernel, groups=groups)
    row = lambda b, c: (b * nc + c, 0)
    return pl.pallas_call(
        kern,
        grid=(B, nc),
        in_specs=[
            pl.BlockSpec((L, C), row),
            pl.BlockSpec((L, C), lambda b, c: (b * nc + c, 1)),
            pl.BlockSpec((1, C), lambda b, c: (0, 0)),
            pl.BlockSpec((1, C), lambda b, c: (0, 0)),
            pl.BlockSpec((groups, L, L), lambda b, c: (0, 0, 0)),
            pl.BlockSpec((L, groups), lambda b, c: (0, 0)),
        ],
        out_specs=[
            pl.BlockSpec((L, C), row),
            pl.BlockSpec((1, L, C), lambda b, c: (b, 0, 0)),
        ],
        out_shape=[
            jax.ShapeDtypeStruct((B * T, C), out_dtype),
            jax.ShapeDtypeStruct((B, L, C), F32),
        ],
        compiler_params=_cparams(("parallel", "arbitrary")),
        name="gmlp_prompt",
    )(z, z, ln_g.reshape(1, C), ln_b.reshape(1, C), ws[:, :L, :L], bs[:, :L].T)


def _gmlp_sample_kernel(u_ref, v_ref, lg_ref, lb_ref, wx_ref, bx_ref, d_ref, gv_ref):
    vv = _layernorm(v_ref[...], lg_ref[...], lb_ref[...])
    gv_ref[0] = vv
    L = vv.shape[0]
    mixed = bx_ref[...]
    for s in range(L):
        ws = wx_ref[s].astype(BF16).astype(F32)
        mixed = mixed + ws * vv[s:s + 1, :].astype(BF16).astype(F32)
    d_ref[...] = u_ref[...] * mixed


def _gmlp_sample(z, Bs, Ts, C, ln_g, ln_b, ws, bs):
    L = Ts
    w = jnp.tril(ws[:, :L, :L])
    wx = jnp.repeat(jnp.transpose(w, (2, 1, 0)), LANES, axis=2)
    bx = jnp.repeat(bs[:, :L].T, LANES, axis=1)
    row = lambda b: (b, 0)
    return pl.pallas_call(
        _gmlp_sample_kernel,
        grid=(Bs,),
        in_specs=[
            pl.BlockSpec((L, C), row),
            pl.BlockSpec((L, C), lambda b: (b, 1)),
            pl.BlockSpec((1, C), lambda b: (0, 0)),
            pl.BlockSpec((1, C), lambda b: (0, 0)),
            pl.BlockSpec((L, L, C), lambda b: (0, 0, 0)),
            pl.BlockSpec((L, C), lambda b: (0, 0)),
        ],
        out_specs=[
            pl.BlockSpec((L, C), row),
            pl.BlockSpec((1, L, C), lambda b: (b, 0, 0)),
        ],
        out_shape=[
            jax.ShapeDtypeStruct((Bs * Ts, C), F32),
            jax.ShapeDtypeStruct((Bs, L, C), F32),
        ],
        compiler_params=_cparams(("parallel",)),
        name="gmlp_sample",
    )(z, z, ln_g.reshape(1, C), ln_b.reshape(1, C), wx, bx)


def _run_group(x, mods, P, sample):
    B, T, D = x.shape
    M = B * T
    C = P["conv_w"].shape[1]
    sbw = P["sb_w"]
    dcw = P["dc_w"]
    gc = P["gmlp_ln_g"].shape[0]
    act = F32 if sample is not None else BF16
    big = sample is None

    def gate_args(g):
        if big:
            return dict(gate=g.reshape(B, 1, D), rows_per_gate=T)
        return dict(gate=jnp.repeat(g, T, axis=0))

    outs = {}
    for layer in range(2):
        sh1, sc1, gt1, sh2, sc2, gt2 = jnp.split(mods[layer], 6, axis=-1)
        h = _norm_mod(x, P["norm_g"][layer, 0], sc1, sh1, act).reshape(M, D)
        x2 = x.reshape(M, D)
        if layer == 0:
            w_in = P["w_in_even"]
            ag = _matmul(h, w_in, n_out=2 * C)
            q = _matmul(h, w_in, col_off=2 * C, n_out=sbw, out_dtype=act)
            k = _matmul(h, w_in, col_off=2 * C + sbw, n_out=sbw)
            v = _matmul(h, w_in, col_off=2 * C + 2 * sbw, n_out=sbw)
            heads = sbw // LANES
            if big:
                a_out, conv_st = _conv_module(ag, B, T, C, None, P["conv_w"], P["conv_b"],
                                              P["conv_ln_g"], P["conv_ln_b"], BF16)
                sb = _sb_prompt(q, k, v, B, T, heads, BF16)
            else:
                a_out, conv_st = _conv_module(ag, B, T, C, sample["state_conv"], P["conv_w"],
                                              P["conv_b"], P["conv_ln_g"], P["conv_ln_b"], F32)
                sb = _sb_sample(q.reshape(B, T, sbw), k.reshape(B, T, sbw), v.reshape(B, T, sbw),
                                sample["cache_sb_k"], sample["cache_sb_v"],
                                sample["page_table"]).reshape(M, sbw)
            outs["conv_st"] = conv_st
            outs["sb_k"] = k.reshape(B, T, heads, LANES)
            outs["sb_v"] = v.reshape(B, T, heads, LANES)
            mix = jnp.concatenate([a_out, sb], axis=-1)
            x2 = _matmul(mix, P["w_out_even"], res=x2, **gate_args(gt1))
        else:
            heads = dcw // (2 * LANES)
            qg = jnp.tile(P["q_norm_g"], dcw // LANES)
            kg = jnp.tile(P["k_norm_g"], dcw // LANES)
            w_in = P["w_in_odd"]
            q = _matmul(h, w_in, n_out=dcw, epilogue="group_rmsnorm", vec=qg, out_dtype=act)
            k = _matmul(h, w_in, col_off=dcw, n_out=dcw, epilogue="group_rmsnorm", vec=kg)
            v = _matmul(h, w_in, col_off=2 * dcw, n_out=dcw)
            z = _matmul(h, w_in, col_off=3 * dcw, n_out=2 * gc, epilogue="gelu")
            if big:
                c_out = _dc_prompt(q, k, v, B, T, heads, P["rel_bias"], P["lam_vecs"],
                                   P["sub_norm_g"], BF16)
                d_out, g_v = _gmlp_prompt(z, B, T, gc, P["gmlp_ln_g"], P["gmlp_ln_b"],
                                          P["gmlp_ws"], P["gmlp_bs"], BF16)
            else:
                c_out = _dc_sample(q.reshape(B, T, dcw), k.reshape(B, T, dcw), v.reshape(B, T, dcw),
                                   sample["cache_dc_k"], sample["cache_dc_v"], sample["page_table"],
                                   P["rel_bias"], P["lam_vecs"], P["sub_norm_g"]).reshape(M, dcw)
                d_out, g_v = _gmlp_sample(z, B, T, gc, P["gmlp_ln_g"], P["gmlp_ln_b"],
                                          P["gmlp_ws"], P["gmlp_bs"])
            outs["dc_k"] = k.reshape(B, T, heads, 2, LANES)
            outs["dc_v"] = v.reshape(B, T, heads, 2 * LANES)
            outs["g_v"] = g_v
            mix = jnp.concatenate([c_out, d_out], axis=-1)
            x2 = _matmul(mix, P["w_out_odd"], res=x2, **gate_args(gt1))
        h = _norm_mod(x2.reshape(B, T, D), P["norm_g"][layer, 1], sc2, sh2, act).reshape(M, D)
        a = _matmul(h, P["w_mlp_in"], layer=layer, epilogue="relu2", out_dtype=BF16)
        x2 = _matmul(a, P["w_mlp_out"], layer=layer, res=x2, **gate_args(gt2))
        x = x2.reshape(B, T, D)
    return x, outs


def kernel(x_prompt, x_sample, state_conv, cache_sb_k, cache_sb_v, cache_dc_k, cache_dc_v, page_table, c_prompt, c_sample, norm_g, w_ada, b_ada, w_in_even, w_out_even, conv_w, conv_b, conv_ln_g, conv_ln_b, w_in_odd, w_out_odd, q_norm_g, k_norm_g, lambda_q1, lambda_k1, lambda_q2, lambda_k2, sub_norm_g, rel_bias, gmlp_ln_g, gmlp_ln_b, gmlp_ws, gmlp_bs, w_mlp_in, w_mlp_out):
    B, T, D = x_prompt.shape
    Bs, Ts, _ = x_sample.shape
    sbw = cache_sb_k.shape[2] * cache_sb_k.shape[3]
    dcw = cache_dc_v.shape[2] * cache_dc_v.shape[3]

    P = dict(
        norm_g=norm_g, conv_w=conv_w, conv_b=conv_b, conv_ln_g=conv_ln_g, conv_ln_b=conv_ln_b,
        q_norm_g=q_norm_g, k_norm_g=k_norm_g, sub_norm_g=sub_norm_g, rel_bias=rel_bias,
        gmlp_ln_g=gmlp_ln_g, gmlp_ln_b=gmlp_ln_b, gmlp_ws=gmlp_ws, gmlp_bs=gmlp_bs,
        lam_vecs=jnp.stack([lambda_q1, lambda_k1, lambda_q2, lambda_k2]).astype(F32),
        w_in_even=w_in_even.astype(BF16)[None], w_out_even=w_out_even.astype(BF16)[None],
        w_in_odd=w_in_odd.astype(BF16)[None], w_out_odd=w_out_odd.astype(BF16)[None],
        w_mlp_in=w_mlp_in.astype(BF16), w_mlp_out=w_mlp_out.astype(BF16),
        sb_w=sbw, dc_w=dcw,
    )

    nc = B + Bs
    rows = -(-nc // 16) * 16
    c_all = jnp.pad(jnp.concatenate([c_prompt, c_sample], axis=0), ((0, rows - nc), (0, 0)))
    mods = [_matmul(c_all, w_ada, layer=l, a_silu=True, epilogue="bias", vec=b_ada[l], tn_pref=512)
            for l in range(w_ada.shape[0])]
    mods_p = [m[:B] for m in mods]
    mods_s = [m[B:nc] for m in mods]

    y_p, o_p = _run_group(x_prompt, mods_p, P, None)
    sample = dict(state_conv=state_conv, page_table=page_table, cache_sb_k=cache_sb_k,
                  cache_sb_v=cache_sb_v, cache_dc_k=cache_dc_k, cache_dc_v=cache_dc_v)
    y_s, o_s = _run_group(x_sample, mods_s, P, sample)
    return (y_p, y_s, o_p["conv_st"], o_s["conv_st"],
            o_p["sb_k"], o_p["sb_v"], o_s["sb_k"], o_s["sb_v"],
            o_p["dc_k"], o_p["dc_v"], o_s["dc_k"], o_s["dc_v"],
            o_p["g_v"], o_s["g_v"])
```

```python
import functools
import math

import jax
import jax.numpy as jnp
from jax import lax
from jax.experimental import pallas as pl
from jax.experimental.pallas import tpu as pltpu

F32 = jnp.float32
BF16 = jnp.bfloat16

EPS = 1e-6
LAMBDA_INIT = 0.8 - 0.6 * math.exp(-0.3 * 1)
MAX_DISTANCE = 128
CONV_PAD = 32
LANES = 128
NEG = -1e30
SB_EXIT = -104.0

VMEM_LIMIT = 56 * 1024 * 1024


def _cparams(sem):
    return pltpu.CompilerParams(dimension_semantics=sem, vmem_limit_bytes=VMEM_LIMIT)


def _tile(dim, pref, align):
    if dim <= pref:
        return dim
    t = (pref // align) * align
    while t >= align:
        if dim % t == 0:
            return t
        t -= align
    return dim


def _norm_mod_kernel(x_ref, g_ref, sc_ref, sh_ref, o_ref):
    x = x_ref[0]
    ms = jnp.mean(x * x, axis=-1, keepdims=True)
    y = x * lax.rsqrt(ms + EPS) * g_ref[...]
    o_ref[0] = (y * (1.0 + sc_ref[0]) + sh_ref[0]).astype(o_ref.dtype)


def _norm_mod(x, g, sc, sh, out_dtype):
    B, T, D = x.shape
    tr = _tile(T, 256, 16)
    return pl.pallas_call(
        _norm_mod_kernel,
        grid=(B, T // tr),
        in_specs=[
            pl.BlockSpec((1, tr, D), lambda b, t: (b, t, 0)),
            pl.BlockSpec((1, D), lambda b, t: (0, 0)),
            pl.BlockSpec((1, 1, D), lambda b, t: (b, 0, 0)),
            pl.BlockSpec((1, 1, D), lambda b, t: (b, 0, 0)),
        ],
        out_specs=pl.BlockSpec((1, tr, D), lambda b, t: (b, t, 0)),
        out_shape=jax.ShapeDtypeStruct((B, T, D), out_dtype),
        compiler_params=_cparams(("parallel", "parallel")),
        name="norm_mod",
    )(x, g.reshape(1, D), sc.reshape(B, 1, D), sh.reshape(B, 1, D))


def _gelu_tanh(x):
    c = math.sqrt(2.0 / math.pi)
    return x * (0.5 * (1.0 + jnp.tanh(c * (x + 0.044715 * (x * x * x)))))


def _mm_kernel(*refs, nk, epilogue, a_silu, has_vec, has_res):
    it = iter(refs)
    a_ref = next(it)
    w_ref = next(it)
    vec_ref = next(it) if has_vec else None
    res_ref = next(it) if has_res else None
    gate_ref = next(it) if has_res else None
    o_ref = next(it)
    acc_ref = next(it) if nk > 1 else None

    w = w_ref[...].astype(BF16)
    a = a_ref[...]
    if a_silu:
        a = a * jax.nn.sigmoid(a)
    part = jnp.dot(a.astype(BF16), w, preferred_element_type=F32)

    def finish(acc):
        if epilogue == "relu2":
            r = jnp.maximum(acc, 0.0)
            out = r * r
        elif epilogue == "gelu":
            out = _gelu_tanh(acc)
        elif epilogue == "bias":
            out = acc + vec_ref[...]
        elif epilogue == "group_rmsnorm":
            tn = acc.shape[1]
            for g in range(tn // LANES):
                sl = slice(g * LANES, (g + 1) * LANES)
                x = acc[:, sl]
                ms = jnp.mean(x * x, axis=-1, keepdims=True)
                o_ref[:, sl] = (x * lax.rsqrt(ms + EPS) * vec_ref[:, sl]).astype(o_ref.dtype)
            return
        else:
            out = acc
        if has_res:
            gate = gate_ref[0] if len(gate_ref.shape) == 3 else gate_ref[...]
            out = res_ref[...] + gate * out
        o_ref[...] = out.astype(o_ref.dtype)

    if nk == 1:
        finish(part)
    else:
        k = pl.program_id(2)

        @pl.when(k == 0)
        def _():
            acc_ref[...] = part

        @pl.when(k > 0)
        def _():
            acc_ref[...] += part

        @pl.when(k == nk - 1)
        def _():
            finish(acc_ref[...])


def _matmul(a, w, *, layer=0, col_off=0, n_out=None, epilogue="none", a_silu=False,
            vec=None, res=None, gate=None, rows_per_gate=None, out_dtype=F32, tn_pref=1024):
    M, K = a.shape
    n_out = w.shape[2] - col_off if n_out is None else n_out
    tm = _tile(M if rows_per_gate is None else rows_per_gate, 1024, 16)
    tk = _tile(K, 4096, LANES)
    nk = K // tk
    if res is not None:
        tn_pref = min(tn_pref, 512)
    tn = _tile(math.gcd(n_out, col_off), tn_pref, LANES)
    assert col_off % tn == 0
    joff = col_off // tn
    grid = (M // tm, n_out // tn, nk)
    spec = pl.BlockSpec

    in_specs = [
        spec((tm, tk), lambda i, j, k: (i, k)),
        spec((None, tk, tn), lambda i, j, k: (layer, k, j + joff)),
    ]
    args = [a, w]
    if vec is not None:
        in_specs.append(spec((1, tn), lambda i, j, k: (0, j)))
        args.append(vec.reshape(1, n_out))
    if res is not None:
        in_specs.append(spec((tm, tn), lambda i, j, k: (i, j)))
        args.append(res)
        if gate.ndim == 3:
            assert rows_per_gate % tm == 0
            rpg = rows_per_gate // tm
            in_specs.append(spec((1, 1, tn), lambda i, j, k: (i // rpg, 0, j)))
        else:
            in_specs.append(spec((tm, tn), lambda i, j, k: (i, j)))
        args.append(gate)
    scratch = [pltpu.VMEM((tm, tn), F32)] if nk > 1 else []
    kern = functools.partial(_mm_kernel, nk=nk, epilogue=epilogue, a_silu=a_silu,
                             has_vec=vec is not None, has_res=res is not None)
    return pl.pallas_call(
        kern,
        grid=grid,
        in_specs=in_specs,
        out_specs=spec((tm, tn), lambda i, j, k: (i, j)),
        out_shape=jax.ShapeDtypeStruct((M, n_out), out_dtype),
        scratch_shapes=scratch,
        compiler_params=_cparams(("parallel", "parallel", "arbitrary")),
        name="mm_" + epilogue,
    )(*args)


def _conv_kernel(val_ref, gate_ref, pre_ref, w_ref, b_ref, lg_ref, lb_ref, o_ref, st_ref,
                 xx_ref, cv_ref, *, tr, width, rc, cc):
    t = pl.program_id(1)
    nt = pl.num_programs(1)
    C = val_ref.shape[1]

    @pl.when(t == 0)
    def _():
        xx_ref[0:CONV_PAD, :] = pre_ref[0]

    xx_ref[CONV_PAD:CONV_PAD + tr, :] = val_ref[...] * jax.nn.sigmoid(gate_ref[...])

    base = CONV_PAD - (width - 1)
    sub = 8
    for r0 in range(0, tr, rc):
        for c0 in range(0, C, cc):
            acc = jnp.zeros((rc, cc), F32)
            for s in range(sub):
                rows = rc + (sub if s else 0)
                part = None
                for u in range(s, base + width, sub):
                    if u < base:
                        continue
                    term = (w_ref[u - base:u - base + 1, c0:c0 + cc]
                            * xx_ref[r0 + u - s:r0 + u - s + rows, c0:c0 + cc])
                    part = term if part is None else part + term
                acc = acc + part[s:s + rc]
            cv_ref[r0:r0 + rc, c0:c0 + cc] = acc + b_ref[:, c0:c0 + cc]

    cv = cv_ref[...]
    mu = jnp.mean(cv, axis=-1, keepdims=True)
    xc = cv - mu
    var = jnp.mean(xc * xc, axis=-1, keepdims=True)
    y = xc * lax.rsqrt(var + EPS) * lg_ref[...] + lb_ref[...]
    o_ref[...] = (y * jax.nn.sigmoid(y)).astype(o_ref.dtype)

    @pl.when(t == nt - 1)
    def _():
        st_ref[0] = xx_ref[tr:tr + CONV_PAD, :]

    @pl.when(t < nt - 1)
    def _():
        xx_ref[0:CONV_PAD, :] = xx_ref[tr:tr + CONV_PAD, :]


def _conv_module(proj, B, T, C, prefix, conv_w, conv_b, ln_g, ln_b, out_dtype):
    width = conv_w.shape[0]
    hist = width - 1
    assert hist <= CONV_PAD
    tr = _tile(T, 256, 8)
    nt = T // tr
    assert nt == 1 or tr >= CONV_PAD
    if prefix is None:
        pre = jnp.zeros((B, CONV_PAD, C), F32)
    else:
        pre = jnp.pad(prefix.astype(F32), ((0, 0), (CONV_PAD - hist, 0), (0, 0)))
    rc = _tile(tr, 64, 8)
    cc = _tile(C, 256, LANES)
    kern = functools.partial(_conv_kernel, tr=tr, width=width, rc=rc, cc=cc)
    row = lambda b, t: (b * nt + t, 0)
    a_out, st = pl.pallas_call(
        kern,
        grid=(B, nt),
        in_specs=[
            pl.BlockSpec((tr, C), row),
            pl.BlockSpec((tr, C), lambda b, t: (b * nt + t, 1)),
            pl.BlockSpec((1, CONV_PAD, C), lambda b, t: (b, 0, 0)),
            pl.BlockSpec((width, C), lambda b, t: (0, 0)),
            pl.BlockSpec((1, C), lambda b, t: (0, 0)),
            pl.BlockSpec((1, C), lambda b, t: (0, 0)),
            pl.BlockSpec((1, C), lambda b, t: (0, 0)),
        ],
        out_specs=[
            pl.BlockSpec((tr, C), row),
            pl.BlockSpec((1, CONV_PAD, C), lambda b, t: (b, 0, 0)),
        ],
        out_shape=[
            jax.ShapeDtypeStruct((B * T, C), out_dtype),
            jax.ShapeDtypeStruct((B, CONV_PAD, C), F32),
        ],
        scratch_shapes=[pltpu.VMEM((CONV_PAD + tr, C), F32), pltpu.VMEM((tr, C), F32)],
        compiler_params=_cparams(("parallel", "arbitrary")),
        name="conv_module",
    )(proj, proj, pre, conv_w, conv_b.reshape(1, C), ln_g.reshape(1, C), ln_b.reshape(1, C))
    return a_out, st[:, CONV_PAD - hist:, :]


def _dot_nt(a, b):
    return lax.dot_general(a, b, (((1,), (1,)), ((), ())), preferred_element_type=F32)


def _iota2(rows, cols=None):
    shape = (rows, rows if cols is None else cols)
    return lax.broadcasted_iota(jnp.int32, shape, 0), lax.broadcasted_iota(jnp.int32, shape, 1)


def _rep(x, n):
    return x if n == 1 else jnp.tile(x, (1, n))


def _tri_ones(kb):
    r, c = _iota2(kb)
    return jnp.where(r > c, 1.0, 0.0).astype(BF16), jnp.ones((kb, LANES), BF16)


def _sb_block(z, mask, c, tri, ones):
    sp = jnp.maximum(z, 0.0) + jnp.log1p(jnp.exp(-jnp.abs(z)))
    ln = -sp if mask is None else jnp.where(mask, -sp, 0.0)
    hi = ln.astype(BF16)
    lo = (ln - hi.astype(F32)).astype(BF16)
    within = (jnp.dot(hi, tri, preferred_element_type=F32)
              + jnp.dot(lo, tri, preferred_element_type=F32))
    total = (jnp.dot(hi, ones, preferred_element_type=F32)
             + jnp.dot(lo, ones, preferred_element_type=F32))
    a = jnp.exp(z - sp + within + _rep(c, z.shape[1] // LANES))
    if mask is not None:
        a = jnp.where(mask, a, 0.0)
    return a, c + total


def _sb_prompt_kernel(q_ref, k_ref, v_ref, o_ref, c_ref, acc_ref, *, scale, blk, gh):
    i = pl.program_id(2)
    d = LANES
    r, c_io = _iota2(blk)
    tri, ones = _tri_ones(blk)
    qs = [q_ref[:, g * d:(g + 1) * d].astype(BF16) for g in range(gh)]
    c_ref[...] = jnp.zeros_like(c_ref)
    acc_ref[...] = jnp.zeros_like(acc_ref)

    def step(j, mask):
        off = pl.multiple_of(j * blk, blk)
        top = None
        for g in range(gh):
            kj = k_ref[pl.ds(off, blk), g * d:(g + 1) * d].astype(BF16)
            vj = v_ref[pl.ds(off, blk), g * d:(g + 1) * d].astype(BF16)
            z = _dot_nt(qs[g], kj) * scale
            a, c = _sb_block(z, mask, c_ref[g], tri, ones)
            c_ref[g] = c
            acc_ref[g] += jnp.dot(a.astype(BF16), vj, preferred_element_type=F32)
            cmax = jnp.max(c)
            top = cmax if top is None else jnp.maximum(top, cmax)
        return top

    top = step(i, c_io < r)
    lax.while_loop(lambda jt: jnp.logical_and(jt[0] >= 0, jt[1] >= SB_EXIT),
                   lambda jt: (jt[0] - 1, step(jt[0], None)),
                   (i - 1, top))
    for g in range(gh):
        o_ref[:, g * d:(g + 1) * d] = acc_ref[g].astype(o_ref.dtype)


def _sb_prompt(q, k, v, B, T, H, out_dtype):
    blk = _tile(T, 256, LANES) if T % LANES == 0 else T
    nq = T // blk
    d = LANES
    gh = next(g for g in (4, 2, 1) if H % g == 0)
    kern = functools.partial(_sb_prompt_kernel, scale=d ** -0.5, blk=blk, gh=gh)
    return pl.pallas_call(
        kern,
        grid=(B, H // gh, nq),
        in_specs=[
            pl.BlockSpec((blk, gh * d), lambda b, h, i: (b * nq + i, h)),
            pl.BlockSpec((T, gh * d), lambda b, h, i: (b, h)),
            pl.BlockSpec((T, gh * d), lambda b, h, i: (b, h)),
        ],
        out_specs=pl.BlockSpec((blk, gh * d), lambda b, h, i: (b * nq + i, h)),
        out_shape=jax.ShapeDtypeStruct((B * T, H * d), out_dtype),
        scratch_shapes=[pltpu.VMEM((gh, blk, LANES), F32), pltpu.VMEM((gh, blk, d), F32)],
        compiler_params=_cparams(("parallel", "parallel", "arbitrary")),
        name="sb_prompt",
    )(q, k, v)


def _pad_rows(x, rows):
    return jnp.pad(x, ((0, 0), (0, rows - x.shape[1]), (0, 0)))


def _sb_sample_kernel(pt_ref, q_ref, kn_ref, vn_ref, kc_hbm, vc_hbm, o_ref,
                      kbuf, vbuf, sem, c_ref, acc_ref, *, scale, dec, heads, n_pages):
    b = pl.program_id(0)
    d = LANES
    R = dec * heads
    tri, ones = _tri_ones(LANES)
    q = q_ref[0]
    qs = [q[:, h * d:(h + 1) * d].astype(BF16) for h in range(heads)]

    def copies(p, slot):
        page = pt_ref[b, n_pages - 1 - p]
        return (pltpu.make_async_copy(kc_hbm.at[page], kbuf.at[slot], sem.at[0, slot]),
                pltpu.make_async_copy(vc_hbm.at[page], vbuf.at[slot], sem.at[1, slot]))

    def start(p, slot):
        for cp in copies(p, slot):
            cp.start()

    def wait(p, slot):
        for cp in copies(p, slot):
            cp.wait()

    start(0, 0)

    def process(load_k, load_v, mask):
        z = jnp.concatenate([_dot_nt(qs[h], load_k(h)) for h in range(heads)], axis=0) * scale
        a, c = _sb_block(z, mask, c_ref[...], tri, ones)
        c_ref[...] = c
        pv = [jnp.dot(a[h * dec:(h + 1) * dec].astype(BF16), load_v(h), preferred_element_type=F32)
              for h in range(heads)]
        acc_ref[...] += jnp.concatenate(pv, axis=0)
        return jnp.max(c)

    c_ref[...] = jnp.zeros_like(c_ref)
    acc_ref[...] = jnp.zeros_like(acc_ref)
    rr, cc = _iota2(R, LANES)
    top = process(lambda h: kn_ref[0, :, h * d:(h + 1) * d].astype(BF16),
                  lambda h: vn_ref[0, :, h * d:(h + 1) * d].astype(BF16),
                  cc < jnp.bitwise_and(rr, dec - 1))

    def body(carry):
        p, _ = carry
        slot = jnp.bitwise_and(p, 1)
        wait(p, slot)

        @pl.when(p + 1 < n_pages)
        def _():
            start(p + 1, 1 - slot)

        top = process(lambda h: kbuf[slot, pl.ds(h, LANES, stride=heads), :].astype(BF16),
                      lambda h: vbuf[slot, pl.ds(h, LANES, stride=heads), :].astype(BF16),
                      None)
        return p + 1, top

    p_end, _ = lax.while_loop(lambda pt: jnp.logical_and(pt[0] < n_pages, pt[1] >= SB_EXIT),
                              body, (jnp.int32(0), top))

    @pl.when(p_end < n_pages)
    def _():
        wait(p_end, jnp.bitwise_and(p_end, 1))

    acc = acc_ref[...]
    for h in range(heads):
        o_ref[0, :, h * d:(h + 1) * d] = acc[h * dec:(h + 1) * dec]


def _sb_sample(q, k_new, v_new, cache_k, cache_v, page_table):
    Bs, Ts, W = q.shape
    n_pool, page, heads, d = cache_k.shape
    n_pages = page_table.shape[1]
    assert page == LANES and d == LANES and Ts & (Ts - 1) == 0
    R = Ts * heads
    kern = functools.partial(_sb_sample_kernel, scale=d ** -0.5, dec=Ts, heads=heads, n_pages=n_pages)
    per_b = lambda b, pt: (b, 0, 0)
    grid_spec = pltpu.PrefetchScalarGridSpec(
        num_scalar_prefetch=1,
        grid=(Bs,),
        in_specs=[
            pl.BlockSpec((1, Ts, W), per_b),
            pl.BlockSpec((1, page, W), per_b),
            pl.BlockSpec((1, page, W), per_b),
            pl.BlockSpec(memory_space=pl.ANY),
            pl.BlockSpec(memory_space=pl.ANY),
        ],
        out_specs=pl.BlockSpec((1, Ts, W), per_b),
        scratch_shapes=[
            pltpu.VMEM((2, page * heads, d), F32), pltpu.VMEM((2, page * heads, d), F32),
            pltpu.SemaphoreType.DMA((2, 2)),
            pltpu.VMEM((R, LANES), F32), pltpu.VMEM((R, d), F32),
        ],
    )
    return pl.pallas_call(
        kern,
        grid_spec=grid_spec,
        out_shape=jax.ShapeDtypeStruct((Bs, Ts, W), F32),
        compiler_params=_cparams(("arbitrary",)),
        name="sb_sample",
    )(page_table, q, _pad_rows(k_new, page), _pad_rows(v_new, page),
      cache_k.reshape(n_pool, page * heads, d), cache_v.reshape(n_pool, page * heads, d))


def _t5_bucket(rel, n_buckets):
    n = jnp.maximum(rel, 0)
    max_exact = n_buckets // 2
    nf = jnp.maximum(n, 1).astype(F32)
    large = max_exact + (jnp.log(nf / max_exact) / math.log(MAX_DISTANCE / max_exact)
                         * (n_buckets - max_exact)).astype(jnp.int32)
    large = jnp.minimum(large, n_buckets - 1)
    return jnp.where(n < max_exact, n, large)


def _bias_table(rel, rel_bias):
    nb = rel_bias.shape[0]
    onehot = jax.nn.one_hot(_t5_bucket(rel, nb), nb, dtype=F32)
    return jnp.einsum("...n,nh->h...", onehot, rel_bias.astype(F32), precision=lax.Precision.HIGHEST)


def _lambda(lv_ref):
    s1 = jnp.sum(lv_ref[0:1, :] * lv_ref[1:2, :], axis=-1, keepdims=True)
    s2 = jnp.sum(lv_ref[2:3, :] * lv_ref[3:4, :], axis=-1, keepdims=True)
    return jnp.exp(s1) - jnp.exp(s2) + LAMBDA_INIT


def _softmax_step(s, m, l, ones):
    m_new = jnp.maximum(m, jnp.max(s, axis=-1, keepdims=True))
    alpha = jnp.exp(m - m_new)
    p = jnp.exp(s - _rep(m_new, s.shape[1] // LANES)).astype(BF16)
    return p, alpha, m_new, alpha * l + jnp.dot(p, ones, preferred_element_type=F32)


def _sub_norm(x, g):
    ms = jnp.mean(x * x, axis=-1, keepdims=True)
    return x * lax.rsqrt(ms + EPS) * g * (1.0 - LAMBDA_INIT)


def _dc_prompt_kernel(q_ref, k_ref, v_ref, tab_ref, far_ref, lv_ref, g_ref, o_ref,
                      m_ref, l_ref, acc_ref, *, scale, blk, hp):
    i = pl.program_id(2)
    d = LANES
    d2 = 2 * LANES
    nch = 2 * hp
    q = [q_ref[:, n * d:(n + 1) * d].astype(BF16) for n in range(nch)]
    m_ref[...] = jnp.full_like(m_ref, NEG)
    l_ref[...] = jnp.zeros_like(l_ref)
    acc_ref[...] = jnp.zeros_like(acc_ref)

    def step(pieces):
        offs = [pl.multiple_of(j * blk, blk) for j, _, _ in pieces]
        ones = jnp.ones((len(pieces) * blk, LANES), BF16)
        for hh in range(hp):
            vs = [v_ref[pl.ds(off, blk), hh * d2:(hh + 1) * d2].astype(BF16) for off in offs]
            for c in range(2):
                n = 2 * hh + c
                parts = []
                for off, (_, bias_fn, mask) in zip(offs, pieces):
                    kj = k_ref[pl.ds(off, blk), n * d:(n + 1) * d].astype(BF16)
                    sp = _dot_nt(q[n], kj) * scale + bias_fn(hh)
                    parts.append(sp if mask is None else jnp.where(mask, sp, NEG))
                s = parts[0] if len(parts) == 1 else jnp.concatenate(parts, axis=1)
                p, alpha, m_new, l_new = _softmax_step(s, m_ref[n], l_ref[n], ones)
                m_ref[n] = m_new
                l_ref[n] = l_new
                pv = None
                for t, vj in enumerate(vs):
                    pt = jnp.dot(p[:, t * blk:(t + 1) * blk], vj, preferred_element_type=F32)
                    pv = pt if pv is None else pv + pt
                acc_ref[n] = _rep(alpha, 2) * acc_ref[n] + pv

    nfar = jnp.maximum(i - 1, 0)
    far = lambda hh: far_ref[hh]
    tab = lambda idx: (lambda hh: tab_ref[hh, idx])

    def far_body(t, carry):
        step([(2 * t, far, None), (2 * t + 1, far, None)])
        return carry

    lax.fori_loop(0, jnp.right_shift(nfar, 1), far_body, 0)

    @pl.when(jnp.bitwise_and(nfar, 1) == 1)
    def _():
        step([(nfar - 1, far, None)])

    r, c_io = _iota2(blk)
    causal = c_io <= r

    @pl.when(i >= 1)
    def _():
        step([(i - 1, tab(1), None), (i, tab(0), causal)])

    @pl.when(i == 0)
    def _():
        step([(i, tab(0), causal)])

    lam = _lambda(lv_ref)
    for hh in range(hp):
        out = (acc_ref[2 * hh] / _rep(l_ref[2 * hh], 2)
               - lam * (acc_ref[2 * hh + 1] / _rep(l_ref[2 * hh + 1], 2)))
        o_ref[:, hh * d2:(hh + 1) * d2] = _sub_norm(out, g_ref[...]).astype(o_ref.dtype)


def _dc_prompt(q, k, v, B, T, H, rel_bias, lam_vecs, sub_g, out_dtype):
    blk = _tile(T, 256, LANES) if T % LANES == 0 else T
    nq = T // blk
    d2 = 2 * LANES
    hp = 2 if H % 2 == 0 else 1
    assert blk >= MAX_DISTANCE or nq == 1
    r = jnp.arange(blk, dtype=jnp.int32)
    rel0 = r[:, None] - r[None, :]
    tabs = jnp.stack([_bias_table(rel0, rel_bias), _bias_table(rel0 + blk, rel_bias)], axis=1)
    far = jnp.broadcast_to(rel_bias[-1].astype(F32)[:, None, None], (H, 1, blk))
    kern = functools.partial(_dc_prompt_kernel, scale=LANES ** -0.5, blk=blk, hp=hp)
    return pl.pallas_call(
        kern,
        grid=(B, H // hp, nq),
        in_specs=[
            pl.BlockSpec((blk, hp * d2), lambda b, h, i: (b * nq + i, h)),
            pl.BlockSpec((T, hp * d2), lambda b, h, i: (b, h)),
            pl.BlockSpec((T, hp * d2), lambda b, h, i: (b, h)),
            pl.BlockSpec((hp, 2, blk, blk), lambda b, h, i: (h, 0, 0, 0)),
            pl.BlockSpec((hp, 1, blk), lambda b, h, i: (h, 0, 0)),
            pl.BlockSpec((4, LANES), lambda b, h, i: (0, 0)),
            pl.BlockSpec((1, d2), lambda b, h, i: (0, 0)),
        ],
        out_specs=pl.BlockSpec((blk, hp * d2), lambda b, h, i: (b * nq + i, h)),
        out_shape=jax.ShapeDtypeStruct((B * T, H * d2), out_dtype),
        scratch_shapes=[pltpu.VMEM((2 * hp, blk, LANES), F32), pltpu.VMEM((2 * hp, blk, LANES), F32),
                        pltpu.VMEM((2 * hp, blk, d2), F32)],
        compiler_params=_cparams(("parallel", "parallel", "arbitrary")),
        name="dc_prompt",
    )(q, k, v, tabs, far, lam_vecs, sub_g.reshape(1, d2))


def _dc_sample_kernel(pt_ref, q_ref, kn_ref, vn_ref, *rest, scale, dec, heads, n_pages, ppb):
    kc_refs = rest[:ppb]
    vc_refs = rest[ppb:2 * ppb]
    (bnew_ref, blast_ref, bfar_ref, lv_ref, g_ref, o_ref,
     m_ref, l_ref, acc_ref, spread_ref, own_ref) = rest[2 * ppb:]
    s = pl.program_id(1)
    ns = pl.num_programs(1)
    d = LANES
    d2 = 2 * LANES
    groups = 2 * heads
    R = dec * groups
    dshift = dec.bit_length() - 1
    hshift = heads.bit_length() - 1
    q = q_ref[0]
    qs = [q[:, g * d:(g + 1) * d].astype(BF16) for g in range(groups)]
    ones = jnp.ones((LANES, LANES), BF16)

    def process(load_k, pv_fn, bias, mask):
        sc = jnp.concatenate([_dot_nt(qs[g], load_k(g)) for g in range(groups)], axis=0) * scale + bias
        if mask is not None:
            sc = jnp.where(mask, sc, NEG)
        p, alpha, m_new, l_new = _softmax_step(sc, m_ref[...], l_ref[...], ones)
        m_ref[...] = m_new
        l_ref[...] = l_new
        acc_ref[...] = _rep(alpha, 2) * acc_ref[...] + pv_fn(p)

    @pl.when(s == 0)
    def _():
        m_ref[...] = jnp.full_like(m_ref, NEG)
        l_ref[...] = jnp.zeros_like(l_ref)
        acc_ref[...] = jnp.zeros_like(acc_ref)
        kk, jj = _iota2(LANES, LANES * heads)
        spread_ref[...] = jnp.where(jnp.right_shift(jj, hshift) == kk, 1.0, 0.0).astype(BF16)
        rr, jj = _iota2(R, LANES * heads)
        own_ref[...] = jnp.where(jnp.right_shift(rr, dshift + 1) == jnp.bitwise_and(jj, heads - 1),
                                 1.0, 0.0).astype(BF16)

        def pv_new(p):
            return jnp.concatenate(
                [jnp.dot(p[h * 2 * dec:(h + 1) * 2 * dec],
                         vn_ref[0, :, h * d2:(h + 1) * d2].astype(BF16), preferred_element_type=F32)
                 for h in range(heads)], axis=0)

        rr, cc = _iota2(R, LANES)
        process(lambda g: kn_ref[0, :, g * d:(g + 1) * d].astype(BF16), pv_new,
                bnew_ref[...], cc <= jnp.bitwise_and(rr, dec - 1))

    sc = jnp.concatenate(
        [jnp.concatenate([_dot_nt(qs[g], kc_refs[u][0, pl.ds(g, LANES, stride=groups), :].astype(BF16))
                          for g in range(groups)], axis=0) for u in range(ppb)], axis=1)
    bias = jnp.concatenate([jnp.where(s * ppb + u == n_pages - 1, blast_ref[...], bfar_ref[...])
                            for u in range(ppb)], axis=1)
    p, alpha, m_new, l_new = _softmax_step(sc * scale + bias, m_ref[...], l_ref[...],
                                           jnp.ones((ppb * LANES, LANES), BF16))
    m_ref[...] = m_new
    l_ref[...] = l_new
    pv = None
    for u in range(ppb):
        pe = jnp.dot(p[:, u * LANES:(u + 1) * LANES], spread_ref[...], preferred_element_type=F32)
        pu = jnp.dot(pe.astype(BF16) * own_ref[...], vc_refs[u][0].astype(BF16), preferred_element_type=F32)
        pv = pu if pv is None else pv + pu
    acc_ref[...] = _rep(alpha, 2) * acc_ref[...] + pv

    @pl.when(s == ns - 1)
    def _():
        lam = _lambda(lv_ref)
        on = acc_ref[...] / _rep(l_ref[...], 2)
        for h in range(heads):
            r0 = h * 2 * dec
            out = on[r0:r0 + dec] - lam * on[r0 + dec:r0 + 2 * dec]
            o_ref[0, :, h * d2:(h + 1) * d2] = _sub_norm(out, g_ref[...])


def _dc_sample(q, k_new, v_new, cache_k, cache_v, page_table, rel_bias, lam_vecs, sub_g):
    Bs, Ts, W = q.shape
    n_pool, page, heads, d2 = cache_v.shape
    n_pages = page_table.shape[1]
    past = n_pages * page
    assert page == LANES and page >= MAX_DISTANCE and d2 == 2 * LANES
    assert Ts & (Ts - 1) == 0 and heads & (heads - 1) == 0
    ppb = next(p for p in (8, 4, 2, 1) if n_pages % p == 0)
    R = Ts * 2 * heads
    tok = jnp.arange(Ts, dtype=jnp.int32)[:, None]
    col = jnp.arange(page, dtype=jnp.int32)[None, :]
    rows = lambda t: jnp.broadcast_to(t[:, None], (heads, 2, Ts, page)).reshape(R, page)
    b_new = rows(_bias_table(tok - col, rel_bias))
    b_last = rows(_bias_table(past + tok - ((n_pages - 1) * page + col), rel_bias))
    b_far = rows(jnp.broadcast_to(rel_bias[-1].astype(F32)[:, None, None], (heads, Ts, page)))
    kern = functools.partial(_dc_sample_kernel, scale=LANES ** -0.5, dec=Ts, heads=heads,
                             n_pages=n_pages, ppb=ppb)
    per_b = lambda b, s, pt: (b, 0, 0)
    const = lambda b, s, pt: (0, 0)
    page_spec = lambda rows_, cols_, u: pl.BlockSpec(
        (1, rows_, cols_), lambda b, s, pt: (pt[b, s * ppb + u], 0, 0))
    grid_spec = pltpu.PrefetchScalarGridSpec(
        num_scalar_prefetch=1,
        grid=(Bs, n_pages // ppb),
        in_specs=[
            pl.BlockSpec((1, Ts, W), per_b),
            pl.BlockSpec((1, page, W), per_b),
            pl.BlockSpec((1, page, W), per_b),
            *[page_spec(page * heads * 2, LANES, u) for u in range(ppb)],
            *[page_spec(page * heads, d2, u) for u in range(ppb)],
            pl.BlockSpec((R, page), const),
            pl.BlockSpec((R, page), const),
            pl.BlockSpec((R, page), const),
            pl.BlockSpec((4, LANES), const),
            pl.BlockSpec((1, d2), const),
        ],
        out_specs=pl.BlockSpec((1, Ts, W), per_b),
        scratch_shapes=[pltpu.VMEM((R, LANES), F32), pltpu.VMEM((R, LANES), F32), pltpu.VMEM((R, d2), F32),
                        pltpu.VMEM((page, page * heads), BF16), pltpu.VMEM((R, page * heads), BF16)],
    )
    ck = cache_k.reshape(n_pool, page * heads * 2, LANES)
    cv = cache_v.reshape(n_pool, page * heads, d2)
    return pl.pallas_call(
        kern,
        grid_spec=grid_spec,
        out_shape=jax.ShapeDtypeStruct((Bs, Ts, W), F32),
        compiler_params=_cparams(("parallel", "arbitrary")),
        name="dc_sample",
    )(page_table, q, _pad_rows(k_new, page), _pad_rows(v_new, page), *([ck] * ppb), *([cv] * ppb),
      b_new, b_last, b_far, lam_vecs, sub_g.reshape(1, d2))


def _layernorm(x, g, b):
    mu = jnp.mean(x, axis=-1, keepdims=True)
    xc = x - mu
    var = jnp.mean(xc * xc, axis=-1, keepdims=True)
    return xc * lax.rsqrt(var + EPS) * g + b


def _gmlp_prompt_kernel(u_ref, v_ref, lg_ref, lb_ref, ws_ref, bst_ref, d_ref, gv_ref, *, groups):
    vv = _layernorm(v_ref[...], lg_ref[...], lb_ref[...])
    gv_ref[0] = vv
    L = vv.shape[0]
    r, c = _iota2(L)
    tril = c <= r
    for g in range(groups):
        sl = slice(g * LANES, (g + 1) * LANES)
        w = jnp.where(tril, ws_ref[g], 0.0).astype(BF16)
        mixed = jnp.dot(w, vv[:, sl].astype(BF16), preferred_element_type=F32) + bst_ref[:, g:g + 1]
        d_ref[:, sl] = (u_ref[:, sl] * mixed).astype(d_ref.dtype)


def _gmlp_prompt(z, B, T, C, ln_g, ln_b, ws, bs, out_dtype):
    L = LANES
    nc = T // L
    groups = C // LANES
    kern = functools.partial(_gmlp_prompt_kernel, groups=groups)
    row = lambda b, c: (b * nc + c, 0)
    return pl.pallas_call(
        kern,
        grid=(B, nc),
        in_specs=[
            pl.BlockSpec((L, C), row),
            pl.BlockSpec((L, C), lambda b, c: (b * nc + c, 1)),
            pl.BlockSpec((1, C), lambda b, c: (0, 0)),
            pl.BlockSpec((1, C), lambda b, c: (0, 0)),
            pl.BlockSpec((groups, L, L), lambda b, c: (0, 0, 0)),
            pl.BlockSpec((L, groups), lambda b, c: (0, 0)),
        ],
        out_specs=[
            pl.BlockSpec((L, C), row),
            pl.BlockSpec((1, L, C), lambda b, c: (b, 0, 0)),
        ],
        out_shape=[
            jax.ShapeDtypeStruct((B * T, C), out_dtype),
            jax.ShapeDtypeStruct((B, L, C), F32),
        ],
        compiler_params=_cparams(("parallel", "arbitrary")),
        name="gmlp_prompt",
    )(z, z, ln_g.reshape(1, C), ln_b.reshape(1, C), ws[:, :L, :L], bs[:, :L].T)


def _gmlp_sample_kernel(u_ref, v_ref, lg_ref, lb_ref, wx_ref, bx_ref, d_ref, gv_ref):
    vv = _layernorm(v_ref[...], lg_ref[...], lb_ref[...])
    gv_ref[0] = vv
    L = vv.shape[0]
    mixed = bx_ref[...]
    for s in range(L):
        ws = wx_ref[s].astype(BF16).astype(F32)
        mixed = mixed + ws * vv[s:s + 1, :].astype(BF16).astype(F32)
    d_ref[...] = u_ref[...] * mixed


def _gmlp_sample(z, Bs, Ts, C, ln_g, ln_b, ws, bs):
    L = Ts
    w = jnp.tril(ws[:, :L, :L])
    wx = jnp.repeat(jnp.transpose(w, (2, 1, 0)), LANES, axis=2)
    bx = jnp.repeat(bs[:, :L].T, LANES, axis=1)
    row = lambda b: (b, 0)
    return pl.pallas_call(
        _gmlp_sample_kernel,
        grid=(Bs,),
        in_specs=[
            pl.BlockSpec((L, C), row),
            pl.BlockSpec((L, C), lambda b: (b, 1)),
            pl.BlockSpec((1, C), lambda b: (0, 0)),
            pl.BlockSpec((1, C), lambda b: (0, 0)),
            pl.BlockSpec((L, L, C), lambda b: (0, 0, 0)),
            pl.BlockSpec((L, C), lambda b: (0, 0)),
        ],
        out_specs=[
            pl.BlockSpec((L, C), row),
            pl.BlockSpec((1, L, C), lambda b: (b, 0, 0)),
        ],
        out_shape=[
            jax.ShapeDtypeStruct((Bs * Ts, C), F32),
            jax.ShapeDtypeStruct((Bs, L, C), F32),
        ],
        compiler_params=_cparams(("parallel",)),
        name="gmlp_sample",
    )(z, z, ln_g.reshape(1, C), ln_b.reshape(1, C), wx, bx)


def _run_group(x, mods, P, sample):
    B, T, D = x.shape
    M = B * T
    C = P["conv_w"].shape[1]
    sbw = P["sb_w"]
    dcw = P["dc_w"]
    gc = P["gmlp_ln_g"].shape[0]
    act = F32 if sample is not None else BF16
    big = sample is None

    def gate_args(g):
        if big:
            return dict(gate=g.reshape(B, 1, D), rows_per_gate=T)
        return dict(gate=jnp.repeat(g, T, axis=0))

    outs = {}
    for layer in range(2):
        sh1, sc1, gt1, sh2, sc2, gt2 = jnp.split(mods[layer], 6, axis=-1)
        h = _norm_mod(x, P["norm_g"][layer, 0], sc1, sh1, act).reshape(M, D)
        x2 = x.reshape(M, D)
        if layer == 0:
            w_in = P["w_in_even"]
            ag = _matmul(h, w_in, n_out=2 * C)
            q = _matmul(h, w_in, col_off=2 * C, n_out=sbw, out_dtype=act)
            k = _matmul(h, w_in, col_off=2 * C + sbw, n_out=sbw)
            v = _matmul(h, w_in, col_off=2 * C + 2 * sbw, n_out=sbw)
            heads = sbw // LANES
            if big:
                a_out, conv_st = _conv_module(ag, B, T, C, None, P["conv_w"], P["conv_b"],
                                              P["conv_ln_g"], P["conv_ln_b"], BF16)
                sb = _sb_prompt(q, k, v, B, T, heads, BF16)
            else:
                a_out, conv_st = _conv_module(ag, B, T, C, sample["state_conv"], P["conv_w"],
                                              P["conv_b"], P["conv_ln_g"], P["conv_ln_b"], F32)
                sb = _sb_sample(q.reshape(B, T, sbw), k.reshape(B, T, sbw), v.reshape(B, T, sbw),
                                sample["cache_sb_k"], sample["cache_sb_v"],
                                sample["page_table"]).reshape(M, sbw)
            outs["conv_st"] = conv_st
            outs["sb_k"] = k.reshape(B, T, heads, LANES)
            outs["sb_v"] = v.reshape(B, T, heads, LANES)
            mix = jnp.concatenate([a_out, sb], axis=-1)
            x2 = _matmul(mix, P["w_out_even"], res=x2, **gate_args(gt1))
        else:
            heads = dcw // (2 * LANES)
            qg = jnp.tile(P["q_norm_g"], dcw // LANES)
            kg = jnp.tile(P["k_norm_g"], dcw // LANES)
            w_in = P["w_in_odd"]
            q = _matmul(h, w_in, n_out=dcw, epilogue="group_rmsnorm", vec=qg, out_dtype=act)
            k = _matmul(h, w_in, col_off=dcw, n_out=dcw, epilogue="group_rmsnorm", vec=kg)
            v = _matmul(h, w_in, col_off=2 * dcw, n_out=dcw)
            z = _matmul(h, w_in, col_off=3 * dcw, n_out=2 * gc, epilogue="gelu")
            if big:
                c_out = _dc_prompt(q, k, v, B, T, heads, P["rel_bias"], P["lam_vecs"],
                                   P["sub_norm_g"], BF16)
                d_out, g_v = _gmlp_prompt(z, B, T, gc, P["gmlp_ln_g"], P["gmlp_ln_b"],
                                          P["gmlp_ws"], P["gmlp_bs"], BF16)
            else:
                c_out = _dc_sample(q.reshape(B, T, dcw), k.reshape(B, T, dcw), v.reshape(B, T, dcw),
                                   sample["cache_dc_k"], sample["cache_dc_v"], sample["page_table"],
                                   P["rel_bias"], P["lam_vecs"], P["sub_norm_g"]).reshape(M, dcw)
                d_out, g_v = _gmlp_sample(z, B, T, gc, P["gmlp_ln_g"], P["gmlp_ln_b"],
                                          P["gmlp_ws"], P["gmlp_bs"])
            outs["dc_k"] = k.reshape(B, T, heads, 2, LANES)
            outs["dc_v"] = v.reshape(B, T, heads, 2 * LANES)
            outs["g_v"] = g_v
            mix = jnp.concatenate([c_out, d_out], axis=-1)
            x2 = _matmul(mix, P["w_out_odd"], res=x2, **gate_args(gt1))
        h = _norm_mod(x2.reshape(B, T, D), P["norm_g"][layer, 1], sc2, sh2, act).reshape(M, D)
        a = _matmul(h, P["w_mlp_in"], layer=layer, epilogue="relu2", out_dtype=BF16)
        x2 = _matmul(a, P["w_mlp_out"], layer=layer, res=x2, **gate_args(gt2))
        x = x2.reshape(B, T, D)
    return x, outs


def kernel(x_prompt, x_sample, state_conv, cache_sb_k, cache_sb_v, cache_dc_k, cache_dc_v, page_table, c_prompt, c_sample, norm_g, w_ada, b_ada, w_in_even, w_out_even, conv_w, conv_b, conv_ln_g, conv_ln_b, w_in_odd, w_out_odd, q_norm_g, k_norm_g, lambda_q1, lambda_k1, lambda_q2, lambda_k2, sub_norm_g, rel_bias, gmlp_ln_g, gmlp_ln_b, gmlp_ws, gmlp_bs, w_mlp_in, w_mlp_out):
    B, T, D = x_prompt.shape
    Bs, Ts, _ = x_sample.shape
    sbw = cache_sb_k.shape[2] * cache_sb_k.shape[3]
    dcw = cache_dc_v.shape[2] * cache_dc_v.shape[3]

    P = dict(
        norm_g=norm_g, conv_w=conv_w, conv_b=conv_b, conv_ln_g=conv_ln_g, conv_ln_b=conv_ln_b,
        q_norm_g=q_norm_g, k_norm_g=k_norm_g, sub_norm_g=sub_norm_g, rel_bias=rel_bias,
        gmlp_ln_g=gmlp_ln_g, gmlp_ln_b=gmlp_ln_b, gmlp_ws=gmlp_ws, gmlp_bs=gmlp_bs,
        lam_vecs=jnp.stack([lambda_q1, lambda_k1, lambda_q2, lambda_k2]).astype(F32),
        w_in_even=w_in_even.astype(BF16)[None], w_out_even=w_out_even.astype(BF16)[None],
        w_in_odd=w_in_odd.astype(BF16)[None], w_out_odd=w_out_odd.astype(BF16)[None],
        w_mlp_in=w_mlp_in.astype(BF16), w_mlp_out=w_mlp_out.astype(BF16),
        sb_w=sbw, dc_w=dcw,
    )

    nc = B + Bs
    rows = -(-nc // 16) * 16
    c_all = jnp.pad(jnp.concatenate([c_prompt, c_sample], axis=0), ((0, rows - nc), (0, 0)))
    mods = [_matmul(c_all, w_ada, layer=l, a_silu=True, epilogue="bias", vec=b_ada[l], tn_pref=512)
            for l in range(w_ada.shape[0])]
    mods_p = [m[:B] for m in mods]
    mods_s = [m[B:nc] for m in mods]

    y_p, o_p = _run_group(x_prompt, mods_p, P, None)
    sample = dict(state_conv=state_conv, page_table=page_table, cache_sb_k=cache_sb_k,
                  cache_sb_v=cache_sb_v, cache_dc_k=cache_dc_k, cache_dc_v=cache_dc_v)
    y_s, o_s = _run_group(x_sample, mods_s, P, sample)
    return (y_p, y_s, o_p["conv_st"], o_s["conv_st"],
            o_p["sb_k"], o_p["sb_v"], o_s["sb_k"], o_s["sb_v"],
            o_p["dc_k"], o_p["dc_v"], o_s["dc_k"], o_s["dc_v"],
            o_p["g_v"], o_s["g_v"])
```

```python
import functools
import math

import jax
import jax.numpy as jnp
from jax import lax
from jax.experimental import pallas as pl
from jax.experimental.pallas import tpu as pltpu

F32 = jnp.float32
BF16 = jnp.bfloat16

EPS = 1e-6
LAMBDA_INIT = 0.8 - 0.6 * math.exp(-0.3 * 1)
MAX_DISTANCE = 128
CONV_PAD = 32
LANES = 128
NEG = -1e30
SB_EXIT = -104.0

VMEM_LIMIT = 56 * 1024 * 1024


def _cparams(sem):
    return pltpu.CompilerParams(dimension_semantics=sem, vmem_limit_bytes=VMEM_LIMIT)


def _tile(dim, pref, align):
    if dim <= pref:
        return dim
    t = (pref // align) * align
    while t >= align:
        if dim % t == 0:
            return t
        t -= align
    return dim


def _norm_mod_kernel(x_ref, g_ref, sc_ref, sh_ref, o_ref):
    x = x_ref[0]
    ms = jnp.mean(x * x, axis=-1, keepdims=True)
    y = x * lax.rsqrt(ms + EPS) * g_ref[...]
    o_ref[0] = (y * (1.0 + sc_ref[0]) + sh_ref[0]).astype(o_ref.dtype)


def _norm_mod(x, g, sc, sh, out_dtype):
    B, T, D = x.shape
    tr = _tile(T, 256, 16)
    return pl.pallas_call(
        _norm_mod_kernel,
        grid=(B, T // tr),
        in_specs=[
            pl.BlockSpec((1, tr, D), lambda b, t: (b, t, 0)),
            pl.BlockSpec((1, D), lambda b, t: (0, 0)),
            pl.BlockSpec((1, 1, D), lambda b, t: (b, 0, 0)),
            pl.BlockSpec((1, 1, D), lambda b, t: (b, 0, 0)),
        ],
        out_specs=pl.BlockSpec((1, tr, D), lambda b, t: (b, t, 0)),
        out_shape=jax.ShapeDtypeStruct((B, T, D), out_dtype),
        compiler_params=_cparams(("parallel", "parallel")),
        name="norm_mod",
    )(x, g.reshape(1, D), sc.reshape(B, 1, D), sh.reshape(B, 1, D))


def _gelu_tanh(x):
    c = math.sqrt(2.0 / math.pi)
    return x * (0.5 * (1.0 + jnp.tanh(c * (x + 0.044715 * (x * x * x)))))


def _mm_kernel(*refs, nk, epilogue, a_silu, has_vec, has_res, has_a2):
    it = iter(refs)
    a_ref = next(it)
    a2_ref = next(it) if has_a2 else None
    w_ref = next(it)
    vec_ref = next(it) if has_vec else None
    res_ref = next(it) if has_res else None
    gate_ref = next(it) if has_res else None
    o_ref = next(it)
    acc_ref = next(it) if nk > 1 else None

    w = w_ref[...].astype(BF16)
    a = a_ref[...]
    if a_silu:
        a = a * jax.nn.sigmoid(a)
    if has_a2:
        ka = a.shape[1]
        part = (jnp.dot(a.astype(BF16), w[:ka], preferred_element_type=F32)
                + jnp.dot(a2_ref[...].astype(BF16), w[ka:], preferred_element_type=F32))
    else:
        part = jnp.dot(a.astype(BF16), w, preferred_element_type=F32)

    def finish(acc):
        if epilogue == "relu2":
            r = jnp.maximum(acc, 0.0)
            out = r * r
        elif epilogue == "gelu":
            out = _gelu_tanh(acc)
        elif epilogue == "bias":
            out = acc + vec_ref[...]
        elif epilogue == "group_rmsnorm":
            tn = acc.shape[1]
            for g in range(tn // LANES):
                sl = slice(g * LANES, (g + 1) * LANES)
                x = acc[:, sl]
                ms = jnp.mean(x * x, axis=-1, keepdims=True)
                o_ref[:, sl] = (x * lax.rsqrt(ms + EPS) * vec_ref[:, sl]).astype(o_ref.dtype)
            return
        else:
            out = acc
        if has_res:
            gate = gate_ref[0] if len(gate_ref.shape) == 3 else gate_ref[...]
            out = res_ref[...] + gate * out
        o_ref[...] = out.astype(o_ref.dtype)

    if nk == 1:
        finish(part)
    else:
        k = pl.program_id(2)

        @pl.when(k == 0)
        def _():
            acc_ref[...] = part

        @pl.when(k > 0)
        def _():
            acc_ref[...] += part

        @pl.when(k == nk - 1)
        def _():
            finish(acc_ref[...])


def _matmul(a, w, *, layer=0, col_off=0, n_out=None, epilogue="none", a_silu=False,
            vec=None, res=None, gate=None, rows_per_gate=None, out_dtype=F32, tn_pref=1024, a2=None):
    M, K = a.shape
    n_out = w.shape[2] - col_off if n_out is None else n_out
    tm = _tile(M if rows_per_gate is None else rows_per_gate, 1024, 16)
    if a2 is not None:
        tk, nk = w.shape[1], 1
        assert tk == K + a2.shape[1] and tk <= 4096
    else:
        tk = _tile(K, 4096, LANES)
        nk = K // tk
    if res is not None:
        tn_pref = min(tn_pref, 512)
    tn = _tile(math.gcd(n_out, col_off), tn_pref, LANES)
    assert col_off % tn == 0
    joff = col_off // tn
    grid = (M // tm, n_out // tn, nk)
    spec = pl.BlockSpec

    if a2 is None:
        in_specs = [spec((tm, tk), lambda i, j, k: (i, k))]
        args = [a]
    else:
        in_specs = [spec((tm, K), lambda i, j, k: (i, 0)), spec((tm, a2.shape[1]), lambda i, j, k: (i, 0))]
        args = [a, a2]
    in_specs.append(spec((None, tk, tn), lambda i, j, k: (layer, k, j + joff)))
    args.append(w)
    if vec is not None:
        in_specs.append(spec((1, tn), lambda i, j, k: (0, j)))
        args.append(vec.reshape(1, n_out))
    if res is not None:
        in_specs.append(spec((tm, tn), lambda i, j, k: (i, j)))
        args.append(res)
        if gate.ndim == 3:
            assert rows_per_gate % tm == 0
            rpg = rows_per_gate // tm
            in_specs.append(spec((1, 1, tn), lambda i, j, k: (i // rpg, 0, j)))
        else:
            in_specs.append(spec((tm, tn), lambda i, j, k: (i, j)))
        args.append(gate)
    scratch = [pltpu.VMEM((tm, tn), F32)] if nk > 1 else []
    kern = functools.partial(_mm_kernel, nk=nk, epilogue=epilogue, a_silu=a_silu,
                             has_vec=vec is not None, has_res=res is not None, has_a2=a2 is not None)
    return pl.pallas_call(
        kern,
        grid=grid,
        in_specs=in_specs,
        out_specs=spec((tm, tn), lambda i, j, k: (i, j)),
        out_shape=jax.ShapeDtypeStruct((M, n_out), out_dtype),
        scratch_shapes=scratch,
        compiler_params=_cparams(("parallel", "parallel", "arbitrary")),
        name="mm_" + epilogue,
    )(*args)


def _conv_kernel(val_ref, gate_ref, pre_ref, w_ref, b_ref, lg_ref, lb_ref, o_ref, st_ref,
                 xx_ref, cv_ref, *, tr, width, rc, cc):
    t = pl.program_id(1)
    nt = pl.num_programs(1)
    C = val_ref.shape[1]

    @pl.when(t == 0)
    def _():
        xx_ref[0:CONV_PAD, :] = pre_ref[0]

    xx_ref[CONV_PAD:CONV_PAD + tr, :] = val_ref[...] * jax.nn.sigmoid(gate_ref[...])

    base = CONV_PAD - (width - 1)
    sub = 8
    for r0 in range(0, tr, rc):
        for c0 in range(0, C, cc):
            acc = jnp.zeros((rc, cc), F32)
            for s in range(sub):
                rows = rc + (sub if s else 0)
                part = None
                for u in range(s, base + width, sub):
                    if u < base:
                        continue
                    term = (w_ref[u - base:u - base + 1, c0:c0 + cc]
                            * xx_ref[r0 + u - s:r0 + u - s + rows, c0:c0 + cc])
                    part = term if part is None else part + term
                acc = acc + part[s:s + rc]
            cv_ref[r0:r0 + rc, c0:c0 + cc] = acc + b_ref[:, c0:c0 + cc]

    cv = cv_ref[...]
    mu = jnp.mean(cv, axis=-1, keepdims=True)
    xc = cv - mu
    var = jnp.mean(xc * xc, axis=-1, keepdims=True)
    y = xc * lax.rsqrt(var + EPS) * lg_ref[...] + lb_ref[...]
    o_ref[...] = (y * jax.nn.sigmoid(y)).astype(o_ref.dtype)

    @pl.when(t == nt - 1)
    def _():
        st_ref[0] = xx_ref[tr:tr + CONV_PAD, :]

    @pl.when(t < nt - 1)
    def _():
        xx_ref[0:CONV_PAD, :] = xx_ref[tr:tr + CONV_PAD, :]


def _conv_module(proj, B, T, C, prefix, conv_w, conv_b, ln_g, ln_b, out_dtype):
    width = conv_w.shape[0]
    hist = width - 1
    assert hist <= CONV_PAD
    tr = _tile(T, 256, 8)
    nt = T // tr
    assert nt == 1 or tr >= CONV_PAD
    if prefix is None:
        pre = jnp.zeros((B, CONV_PAD, C), F32)
    else:
        pre = jnp.pad(prefix.astype(F32), ((0, 0), (CONV_PAD - hist, 0), (0, 0)))
    rc = _tile(tr, 64, 8)
    cc = _tile(C, 256, LANES)
    kern = functools.partial(_conv_kernel, tr=tr, width=width, rc=rc, cc=cc)
    row = lambda b, t: (b * nt + t, 0)
    a_out, st = pl.pallas_call(
        kern,
        grid=(B, nt),
        in_specs=[
            pl.BlockSpec((tr, C), row),
            pl.BlockSpec((tr, C), lambda b, t: (b * nt + t, 1)),
            pl.BlockSpec((1, CONV_PAD, C), lambda b, t: (b, 0, 0)),
            pl.BlockSpec((width, C), lambda b, t: (0, 0)),
            pl.BlockSpec((1, C), lambda b, t: (0, 0)),
            pl.BlockSpec((1, C), lambda b, t: (0, 0)),
            pl.BlockSpec((1, C), lambda b, t: (0, 0)),
        ],
        out_specs=[
            pl.BlockSpec((tr, C), row),
            pl.BlockSpec((1, CONV_PAD, C), lambda b, t: (b, 0, 0)),
        ],
        out_shape=[
            jax.ShapeDtypeStruct((B * T, C), out_dtype),
            jax.ShapeDtypeStruct((B, CONV_PAD, C), F32),
        ],
        scratch_shapes=[pltpu.VMEM((CONV_PAD + tr, C), F32), pltpu.VMEM((tr, C), F32)],
        compiler_params=_cparams(("parallel", "arbitrary")),
        name="conv_module",
    )(proj, proj, pre, conv_w, conv_b.reshape(1, C), ln_g.reshape(1, C), ln_b.reshape(1, C))
    return a_out, st[:, CONV_PAD - hist:, :]


def _dot_nt(a, b):
    return lax.dot_general(a, b, (((1,), (1,)), ((), ())), preferred_element_type=F32)


def _iota2(rows, cols=None):
    shape = (rows, rows if cols is None else cols)
    return lax.broadcasted_iota(jnp.int32, shape, 0), lax.broadcasted_iota(jnp.int32, shape, 1)


def _rep(x, n):
    return x if n == 1 else jnp.tile(x, (1, n))


def _tri_ones(kb):
    r, c = _iota2(kb)
    return jnp.where(r > c, 1.0, 0.0).astype(BF16), jnp.ones((kb, LANES), BF16)


def _sb_block(z, mask, c, tri, ones):
    sp = jnp.maximum(z, 0.0) + jnp.log1p(jnp.exp(-jnp.abs(z)))
    ln = -sp if mask is None else jnp.where(mask, -sp, 0.0)
    hi = ln.astype(BF16)
    lo = (ln - hi.astype(F32)).astype(BF16)
    within = (jnp.dot(hi, tri, preferred_element_type=F32)
              + jnp.dot(lo, tri, preferred_element_type=F32))
    total = (jnp.dot(hi, ones, preferred_element_type=F32)
             + jnp.dot(lo, ones, preferred_element_type=F32))
    a = jnp.exp(z - sp + within + _rep(c, z.shape[1] // LANES))
    if mask is not None:
        a = jnp.where(mask, a, 0.0)
    return a, c + total


def _sb_prompt_kernel(q_ref, k_ref, v_ref, o_ref, c_ref, acc_ref, *, scale, blk, gh):
    i = pl.program_id(2)
    d = LANES
    r, c_io = _iota2(blk)
    tri, ones = _tri_ones(blk)
    qs = [q_ref[:, g * d:(g + 1) * d].astype(BF16) for g in range(gh)]
    c_ref[...] = jnp.zeros_like(c_ref)
    acc_ref[...] = jnp.zeros_like(acc_ref)

    def step(j, mask):
        off = pl.multiple_of(j * blk, blk)
        top = None
        for g in range(gh):
            kj = k_ref[pl.ds(off, blk), g * d:(g + 1) * d].astype(BF16)
            vj = v_ref[pl.ds(off, blk), g * d:(g + 1) * d].astype(BF16)
            z = _dot_nt(qs[g], kj) * scale
            a, c = _sb_block(z, mask, c_ref[g], tri, ones)
            c_ref[g] = c
            acc_ref[g] += jnp.dot(a.astype(BF16), vj, preferred_element_type=F32)
            cmax = jnp.max(c)
            top = cmax if top is None else jnp.maximum(top, cmax)
        return top

    top = step(i, c_io < r)
    lax.while_loop(lambda jt: jnp.logical_and(jt[0] >= 0, jt[1] >= SB_EXIT),
                   lambda jt: (jt[0] - 1, step(jt[0], None)),
                   (i - 1, top))
    for g in range(gh):
        o_ref[:, g * d:(g + 1) * d] = acc_ref[g].astype(o_ref.dtype)


def _sb_prompt(q, k, v, B, T, H, out_dtype):
    blk = _tile(T, 256, LANES) if T % LANES == 0 else T
    nq = T // blk
    d = LANES
    gh = next(g for g in (4, 2, 1) if H % g == 0)
    kern = functools.partial(_sb_prompt_kernel, scale=d ** -0.5, blk=blk, gh=gh)
    return pl.pallas_call(
        kern,
        grid=(B, H // gh, nq),
        in_specs=[
            pl.BlockSpec((blk, gh * d), lambda b, h, i: (b * nq + i, h)),
            pl.BlockSpec((T, gh * d), lambda b, h, i: (b, h)),
            pl.BlockSpec((T, gh * d), lambda b, h, i: (b, h)),
        ],
        out_specs=pl.BlockSpec((blk, gh * d), lambda b, h, i: (b * nq + i, h)),
        out_shape=jax.ShapeDtypeStruct((B * T, H * d), out_dtype),
        scratch_shapes=[pltpu.VMEM((gh, blk, LANES), F32), pltpu.VMEM((gh, blk, d), F32)],
        compiler_params=_cparams(("parallel", "parallel", "arbitrary")),
        name="sb_prompt",
    )(q, k, v)


def _pad_rows(x, rows):
    return jnp.pad(x, ((0, 0), (0, rows - x.shape[1]), (0, 0)))


def _sb_sample_kernel(pt_ref, q_ref, kn_ref, vn_ref, kc_hbm, vc_hbm, o_ref,
                      kbuf, vbuf, sem, c_ref, acc_ref, *, scale, dec, heads, n_pages):
    b = pl.program_id(0)
    d = LANES
    R = dec * heads
    tri, ones = _tri_ones(LANES)
    q = q_ref[0]
    qs = [q[:, h * d:(h + 1) * d].astype(BF16) for h in range(heads)]

    def copies(p, slot):
        page = pt_ref[b, n_pages - 1 - p]
        return (pltpu.make_async_copy(kc_hbm.at[page], kbuf.at[slot], sem.at[0, slot]),
                pltpu.make_async_copy(vc_hbm.at[page], vbuf.at[slot], sem.at[1, slot]))

    def start(p, slot):
        for cp in copies(p, slot):
            cp.start()

    def wait(p, slot):
        for cp in copies(p, slot):
            cp.wait()

    start(0, 0)

    def process(load_k, load_v, mask):
        z = jnp.concatenate([_dot_nt(qs[h], load_k(h)) for h in range(heads)], axis=0) * scale
        a, c = _sb_block(z, mask, c_ref[...], tri, ones)
        c_ref[...] = c
        pv = [jnp.dot(a[h * dec:(h + 1) * dec].astype(BF16), load_v(h), preferred_element_type=F32)
              for h in range(heads)]
        acc_ref[...] += jnp.concatenate(pv, axis=0)
        return jnp.max(c)

    c_ref[...] = jnp.zeros_like(c_ref)
    acc_ref[...] = jnp.zeros_like(acc_ref)
    rr, cc = _iota2(R, LANES)
    top = process(lambda h: kn_ref[0, :, h * d:(h + 1) * d].astype(BF16),
                  lambda h: vn_ref[0, :, h * d:(h + 1) * d].astype(BF16),
                  cc < jnp.bitwise_and(rr, dec - 1))

    def body(carry):
        p, _ = carry
        slot = jnp.bitwise_and(p, 1)
        wait(p, slot)

        @pl.when(p + 1 < n_pages)
        def _():
            start(p + 1, 1 - slot)

        top = process(lambda h: kbuf[slot, pl.ds(h, LANES, stride=heads), :].astype(BF16),
                      lambda h: vbuf[slot, pl.ds(h, LANES, stride=heads), :].astype(BF16),
                      None)
        return p + 1, top

    p_end, _ = lax.while_loop(lambda pt: jnp.logical_and(pt[0] < n_pages, pt[1] >= SB_EXIT),
                              body, (jnp.int32(0), top))

    @pl.when(p_end < n_pages)
    def _():
        wait(p_end, jnp.bitwise_and(p_end, 1))

    acc = acc_ref[...]
    for h in range(heads):
        o_ref[0, :, h * d:(h + 1) * d] = acc[h * dec:(h + 1) * dec]


def _sb_sample(q, k_new, v_new, cache_k, cache_v, page_table):
    Bs, Ts, W = q.shape
    n_pool, page, heads, d = cache_k.shape
    n_pages = page_table.shape[1]
    assert page == LANES and d == LANES and Ts & (Ts - 1) == 0
    R = Ts * heads
    kern = functools.partial(_sb_sample_kernel, scale=d ** -0.5, dec=Ts, heads=heads, n_pages=n_pages)
    per_b = lambda b, pt: (b, 0, 0)
    grid_spec = pltpu.PrefetchScalarGridSpec(
        num_scalar_prefetch=1,
        grid=(Bs,),
        in_specs=[
            pl.BlockSpec((1, Ts, W), per_b),
            pl.BlockSpec((1, page, W), per_b),
            pl.BlockSpec((1, page, W), per_b),
            pl.BlockSpec(memory_space=pl.ANY),
            pl.BlockSpec(memory_space=pl.ANY),
        ],
        out_specs=pl.BlockSpec((1, Ts, W), per_b),
        scratch_shapes=[
            pltpu.VMEM((2, page * heads, d), F32), pltpu.VMEM((2, page * heads, d), F32),
            pltpu.SemaphoreType.DMA((2, 2)),
            pltpu.VMEM((R, LANES), F32), pltpu.VMEM((R, d), F32),
        ],
    )
    return pl.pallas_call(
        kern,
        grid_spec=grid_spec,
        out_shape=jax.ShapeDtypeStruct((Bs, Ts, W), F32),
        compiler_params=_cparams(("arbitrary",)),
        name="sb_sample",
    )(page_table, q, _pad_rows(k_new, page), _pad_rows(v_new, page),
      cache_k.reshape(n_pool, page * heads, d), cache_v.reshape(n_pool, page * heads, d))


def _t5_bucket(rel, n_buckets):
    n = jnp.maximum(rel, 0)
    max_exact = n_buckets // 2
    nf = jnp.maximum(n, 1).astype(F32)
    large = max_exact + (jnp.log(nf / max_exact) / math.log(MAX_DISTANCE / max_exact)
                         * (n_buckets - max_exact)).astype(jnp.int32)
    large = jnp.minimum(large, n_buckets - 1)
    return jnp.where(n < max_exact, n, large)


def _bias_table(rel, rel_bias):
    nb = rel_bias.shape[0]
    onehot = jax.nn.one_hot(_t5_bucket(rel, nb), nb, dtype=F32)
    return jnp.einsum("...n,nh->h...", onehot, rel_bias.astype(F32), precision=lax.Precision.HIGHEST)


def _lambda(lv_ref):
    s1 = jnp.sum(lv_ref[0:1, :] * lv_ref[1:2, :], axis=-1, keepdims=True)
    s2 = jnp.sum(lv_ref[2:3, :] * lv_ref[3:4, :], axis=-1, keepdims=True)
    return jnp.exp(s1) - jnp.exp(s2) + LAMBDA_INIT


def _softmax_step(s, m, l, ones):
    m_new = jnp.maximum(m, jnp.max(s, axis=-1, keepdims=True))
    alpha = jnp.exp(m - m_new)
    p = jnp.exp(s - _rep(m_new, s.shape[1] // LANES)).astype(BF16)
    return p, alpha, m_new, alpha * l + jnp.dot(p, ones, preferred_element_type=F32)


def _sub_norm(x, g):
    ms = jnp.mean(x * x, axis=-1, keepdims=True)
    return x * lax.rsqrt(ms + EPS) * g * (1.0 - LAMBDA_INIT)


def _dc_prompt_kernel(q_ref, k_ref, v_ref, tab_ref, far_ref, lv_ref, g_ref, o_ref,
                      m_ref, l_ref, acc_ref, *, scale, blk, hp):
    i = pl.program_id(2)
    d = LANES
    d2 = 2 * LANES
    nch = 2 * hp
    q = [q_ref[:, n * d:(n + 1) * d].astype(BF16) for n in range(nch)]
    m_ref[...] = jnp.full_like(m_ref, NEG)
    l_ref[...] = jnp.zeros_like(l_ref)
    acc_ref[...] = jnp.zeros_like(acc_ref)

    def step(pieces):
        offs = [pl.multiple_of(j * blk, blk) for j, _, _ in pieces]
        ones = jnp.ones((len(pieces) * blk, LANES), BF16)
        for hh in range(hp):
            vs = [v_ref[pl.ds(off, blk), hh * d2:(hh + 1) * d2].astype(BF16) for off in offs]
            for c in range(2):
                n = 2 * hh + c
                parts = []
                for off, (_, bias_fn, mask) in zip(offs, pieces):
                    kj = k_ref[pl.ds(off, blk), n * d:(n + 1) * d].astype(BF16)
                    sp = _dot_nt(q[n], kj) * scale + bias_fn(hh)
                    parts.append(sp if mask is None else jnp.where(mask, sp, NEG))
                s = parts[0] if len(parts) == 1 else jnp.concatenate(parts, axis=1)
                p, alpha, m_new, l_new = _softmax_step(s, m_ref[n], l_ref[n], ones)
                m_ref[n] = m_new
                l_ref[n] = l_new
                pv = None
                for t, vj in enumerate(vs):
                    pt = jnp.dot(p[:, t * blk:(t + 1) * blk], vj, preferred_element_type=F32)
                    pv = pt if pv is None else pv + pt
                acc_ref[n] = _rep(alpha, 2) * acc_ref[n] + pv

    nfar = jnp.maximum(i - 1, 0)
    far = lambda hh: far_ref[hh]
    tab = lambda idx: (lambda hh: tab_ref[hh, idx])

    def far_body(t, carry):
        step([(2 * t, far, None), (2 * t + 1, far, None)])
        return carry

    lax.fori_loop(0, jnp.right_shift(nfar, 1), far_body, 0)

    @pl.when(jnp.bitwise_and(nfar, 1) == 1)
    def _():
        step([(nfar - 1, far, None)])

    r, c_io = _iota2(blk)
    causal = c_io <= r

    @pl.when(i >= 1)
    def _():
        step([(i - 1, tab(1), None), (i, tab(0), causal)])

    @pl.when(i == 0)
    def _():
        step([(i, tab(0), causal)])

    lam = _lambda(lv_ref)
    for hh in range(hp):
        out = (acc_ref[2 * hh] / _rep(l_ref[2 * hh], 2)
               - lam * (acc_ref[2 * hh + 1] / _rep(l_ref[2 * hh + 1], 2)))
        o_ref[:, hh * d2:(hh + 1) * d2] = _sub_norm(out, g_ref[...]).astype(o_ref.dtype)


def _dc_prompt(q, k, v, B, T, H, rel_bias, lam_vecs, sub_g, out_dtype):
    blk = _tile(T, 256, LANES) if T % LANES == 0 else T
    nq = T // blk
    d2 = 2 * LANES
    hp = 2 if H % 2 == 0 else 1
    assert blk >= MAX_DISTANCE or nq == 1
    r = jnp.arange(blk, dtype=jnp.int32)
    rel0 = r[:, None] - r[None, :]
    tabs = jnp.stack([_bias_table(rel0, rel_bias), _bias_table(rel0 + blk, rel_bias)], axis=1)
    far = jnp.broadcast_to(rel_bias[-1].astype(F32)[:, None, None], (H, 1, blk))
    kern = functools.partial(_dc_prompt_kernel, scale=LANES ** -0.5, blk=blk, hp=hp)
    return pl.pallas_call(
        kern,
        grid=(B, H // hp, nq),
        in_specs=[
            pl.BlockSpec((blk, hp * d2), lambda b, h, i: (b * nq + i, h)),
            pl.BlockSpec((T, hp * d2), lambda b, h, i: (b, h)),
            pl.BlockSpec((T, hp * d2), lambda b, h, i: (b, h)),
            pl.BlockSpec((hp, 2, blk, blk), lambda b, h, i: (h, 0, 0, 0)),
            pl.BlockSpec((hp, 1, blk), lambda b, h, i: (h, 0, 0)),
            pl.BlockSpec((4, LANES), lambda b, h, i: (0, 0)),
            pl.BlockSpec((1, d2), lambda b, h, i: (0, 0)),
        ],
        out_specs=pl.BlockSpec((blk, hp * d2), lambda b, h, i: (b * nq + i, h)),
        out_shape=jax.ShapeDtypeStruct((B * T, H * d2), out_dtype),
        scratch_shapes=[pltpu.VMEM((2 * hp, blk, LANES), F32), pltpu.VMEM((2 * hp, blk, LANES), F32),
                        pltpu.VMEM((2 * hp, blk, d2), F32)],
        compiler_params=_cparams(("parallel", "parallel", "arbitrary")),
        name="dc_prompt",
    )(q, k, v, tabs, far, lam_vecs, sub_g.reshape(1, d2))


def _dc_sample_kernel(pt_ref, q_ref, kn_ref, vn_ref, *rest, scale, dec, heads, n_pages, ppb):
    kc_refs = rest[:ppb]
    vc_refs = rest[ppb:2 * ppb]
    (bnew_ref, blast_ref, bfar_ref, lv_ref, g_ref, o_ref,
     m_ref, l_ref, acc_ref, spread_ref, own_ref) = rest[2 * ppb:]
    s = pl.program_id(1)
    ns = pl.num_programs(1)
    d = LANES
    d2 = 2 * LANES
    groups = 2 * heads
    R = dec * groups
    dshift = dec.bit_length() - 1
    hshift = heads.bit_length() - 1
    q = q_ref[0]
    qs = [q[:, g * d:(g + 1) * d].astype(BF16) for g in range(groups)]
    ones = jnp.ones((LANES, LANES), BF16)

    def process(load_k, pv_fn, bias, mask):
        sc = jnp.concatenate([_dot_nt(qs[g], load_k(g)) for g in range(groups)], axis=0) * scale + bias
        if mask is not None:
            sc = jnp.where(mask, sc, NEG)
        p, alpha, m_new, l_new = _softmax_step(sc, m_ref[...], l_ref[...], ones)
        m_ref[...] = m_new
        l_ref[...] = l_new
        acc_ref[...] = _rep(alpha, 2) * acc_ref[...] + pv_fn(p)

    @pl.when(s == 0)
    def _():
        m_ref[...] = jnp.full_like(m_ref, NEG)
        l_ref[...] = jnp.zeros_like(l_ref)
        acc_ref[...] = jnp.zeros_like(acc_ref)
        kk, jj = _iota2(LANES, LANES * heads)
        spread_ref[...] = jnp.where(jnp.right_shift(jj, hshift) == kk, 1.0, 0.0).astype(BF16)
        rr, jj = _iota2(R, LANES * heads)
        own_ref[...] = jnp.where(jnp.right_shift(rr, dshift + 1) == jnp.bitwise_and(jj, heads - 1),
                                 1.0, 0.0).astype(BF16)

        def pv_new(p):
            return jnp.concatenate(
                [jnp.dot(p[h * 2 * dec:(h + 1) * 2 * dec],
                         vn_ref[0, :, h * d2:(h + 1) * d2].astype(BF16), preferred_element_type=F32)
                 for h in range(heads)], axis=0)

        rr, cc = _iota2(R, LANES)
        process(lambda g: kn_ref[0, :, g * d:(g + 1) * d].astype(BF16), pv_new,
                bnew_ref[...], cc <= jnp.bitwise_and(rr, dec - 1))

    sc = jnp.concatenate(
        [jnp.concatenate([_dot_nt(qs[g], kc_refs[u][0, pl.ds(g, LANES, stride=groups), :].astype(BF16))
                          for g in range(groups)], axis=0) for u in range(ppb)], axis=1)
    bias = jnp.concatenate([jnp.where(s * ppb + u == n_pages - 1, blast_ref[...], bfar_ref[...])
                            for u in range(ppb)], axis=1)
    p, alpha, m_new, l_new = _softmax_step(sc * scale + bias, m_ref[...], l_ref[...],
                                           jnp.ones((ppb * LANES, LANES), BF16))
    m_ref[...] = m_new
    l_ref[...] = l_new
    pv = None
    for u in range(ppb):
        pe = jnp.dot(p[:, u * LANES:(u + 1) * LANES], spread_ref[...], preferred_element_type=F32)
        pu = jnp.dot(pe.astype(BF16) * own_ref[...], vc_refs[u][0].astype(BF16), preferred_element_type=F32)
        pv = pu if pv is None else pv + pu
    acc_ref[...] = _rep(alpha, 2) * acc_ref[...] + pv

    @pl.when(s == ns - 1)
    def _():
        lam = _lambda(lv_ref)
        on = acc_ref[...] / _rep(l_ref[...], 2)
        for h in range(heads):
            r0 = h * 2 * dec
            out = on[r0:r0 + dec] - lam * on[r0 + dec:r0 + 2 * dec]
            o_ref[0, :, h * d2:(h + 1) * d2] = _sub_norm(out, g_ref[...])


def _dc_sample(q, k_new, v_new, cache_k, cache_v, page_table, rel_bias, lam_vecs, sub_g):
    Bs, Ts, W = q.shape
    n_pool, page, heads, d2 = cache_v.shape
    n_pages = page_table.shape[1]
    past = n_pages * page
    assert page == LANES and page >= MAX_DISTANCE and d2 == 2 * LANES
    assert Ts & (Ts - 1) == 0 and heads & (heads - 1) == 0
    ppb = next(p for p in (8, 4, 2, 1) if n_pages % p == 0)
    R = Ts * 2 * heads
    tok = jnp.arange(Ts, dtype=jnp.int32)[:, None]
    col = jnp.arange(page, dtype=jnp.int32)[None, :]
    rows = lambda t: jnp.broadcast_to(t[:, None], (heads, 2, Ts, page)).reshape(R, page)
    b_new = rows(_bias_table(tok - col, rel_bias))
    b_last = rows(_bias_table(past + tok - ((n_pages - 1) * page + col), rel_bias))
    b_far = rows(jnp.broadcast_to(rel_bias[-1].astype(F32)[:, None, None], (heads, Ts, page)))
    kern = functools.partial(_dc_sample_kernel, scale=LANES ** -0.5, dec=Ts, heads=heads,
                             n_pages=n_pages, ppb=ppb)
    per_b = lambda b, s, pt: (b, 0, 0)
    const = lambda b, s, pt: (0, 0)
    page_spec = lambda rows_, cols_, u: pl.BlockSpec(
        (1, rows_, cols_), lambda b, s, pt: (pt[b, s * ppb + u], 0, 0))
    grid_spec = pltpu.PrefetchScalarGridSpec(
        num_scalar_prefetch=1,
        grid=(Bs, n_pages // ppb),
        in_specs=[
            pl.BlockSpec((1, Ts, W), per_b),
            pl.BlockSpec((1, page, W), per_b),
            pl.BlockSpec((1, page, W), per_b),
            *[page_spec(page * heads * 2, LANES, u) for u in range(ppb)],
            *[page_spec(page * heads, d2, u) for u in range(ppb)],
            pl.BlockSpec((R, page), const),
            pl.BlockSpec((R, page), const),
            pl.BlockSpec((R, page), const),
            pl.BlockSpec((4, LANES), const),
            pl.BlockSpec((1, d2), const),
        ],
        out_specs=pl.BlockSpec((1, Ts, W), per_b),
        scratch_shapes=[pltpu.VMEM((R, LANES), F32), pltpu.VMEM((R, LANES), F32), pltpu.VMEM((R, d2), F32),
                        pltpu.VMEM((page, page * heads), BF16), pltpu.VMEM((R, page * heads), BF16)],
    )
    ck = cache_k.reshape(n_pool, page * heads * 2, LANES)
    cv = cache_v.reshape(n_pool, page * heads, d2)
    return pl.pallas_call(
        kern,
        grid_spec=grid_spec,
        out_shape=jax.ShapeDtypeStruct((Bs, Ts, W), F32),
        compiler_params=_cparams(("parallel", "arbitrary")),
        name="dc_sample",
    )(page_table, q, _pad_rows(k_new, page), _pad_rows(v_new, page), *([ck] * ppb), *([cv] * ppb),
      b_new, b_last, b_far, lam_vecs, sub_g.reshape(1, d2))


def _layernorm(x, g, b):
    mu = jnp.mean(x, axis=-1, keepdims=True)
    xc = x - mu
    var = jnp.mean(xc * xc, axis=-1, keepdims=True)
    return xc * lax.rsqrt(var + EPS) * g + b


def _gmlp_prompt_kernel(u_ref, v_ref, lg_ref, lb_ref, ws_ref, bst_ref, d_ref, gv_ref, *, groups):
    vv = _layernorm(v_ref[...], lg_ref[...], lb_ref[...])
    gv_ref[0] = vv
    L = vv.shape[0]
    r, c = _iota2(L)
    tril = c <= r
    for g in range(groups):
        sl = slice(g * LANES, (g + 1) * LANES)
        w = jnp.where(tril, ws_ref[g], 0.0).astype(BF16)
        mixed = jnp.dot(w, vv[:, sl].astype(BF16), preferred_element_type=F32) + bst_ref[:, g:g + 1]
        d_ref[:, sl] = (u_ref[:, sl] * mixed).astype(d_ref.dtype)


def _gmlp_prompt(z, B, T, C, ln_g, ln_b, ws, bs, out_dtype):
    L = LANES
    nc = T // L
    groups = C // LANES
    kern = functools.partial(_gmlp_prompt_kernel, groups=groups)
    row = lambda b, c: (b * nc + c, 0)
    return pl.pallas_call(
        kern,
        grid=(B, nc),
        in_specs=[
            pl.BlockSpec((L, C), row),
            pl.BlockSpec((L, C), lambda b, c: (b * nc + c, 1)),
            pl.BlockSpec((1, C), lambda b, c: (0, 0)),
            pl.BlockSpec((1, C), lambda b, c: (0, 0)),
            pl.BlockSpec((groups, L, L), lambda b, c: (0, 0, 0)),
            pl.BlockSpec((L, groups), lambda b, c: (0, 0)),
        ],
        out_specs=[
            pl.BlockSpec((L, C), row),
            pl.BlockSpec((1, L, C), lambda b, c: (b, 0, 0)),
        ],
        out_shape=[
            jax.ShapeDtypeStruct((B * T, C), out_dtype),
            jax.ShapeDtypeStruct((B, L, C), F32),
        ],
        compiler_params=_cparams(("parallel", "arbitrary")),
        name="gmlp_prompt",
    )(z, z, ln_g.reshape(1, C), ln_b.reshape(1, C), ws[:, :L, :L], bs[:, :L].T)


def _gmlp_sample_kernel(u_ref, v_ref, lg_ref, lb_ref, wx_ref, bx_ref, d_ref, gv_ref):
    vv = _layernorm(v_ref[...], lg_ref[...], lb_ref[...])
    gv_ref[0] = vv
    L = vv.shape[0]
    mixed = bx_ref[...]
    for s in range(L):
        ws = wx_ref[s].astype(BF16).astype(F32)
        mixed = mixed + ws * vv[s:s + 1, :].astype(BF16).astype(F32)
    d_ref[...] = u_ref[...] * mixed


def _gmlp_sample(z, Bs, Ts, C, ln_g, ln_b, ws, bs):
    L = Ts
    w = jnp.tril(ws[:, :L, :L])
    wx = jnp.repeat(jnp.transpose(w, (2, 1, 0)), LANES, axis=2)
    bx = jnp.repeat(bs[:, :L].T, LANES, axis=1)
    row = lambda b: (b, 0)
    return pl.pallas_call(
        _gmlp_sample_kernel,
        grid=(Bs,),
        in_specs=[
            pl.BlockSpec((L, C), row),
            pl.BlockSpec((L, C), lambda b: (b, 1)),
            pl.BlockSpec((1, C), lambda b: (0, 0)),
            pl.BlockSpec((1, C), lambda b: (0, 0)),
            pl.BlockSpec((L, L, C), lambda b: (0, 0, 0)),
            pl.BlockSpec((L, C), lambda b: (0, 0)),
        ],
        out_specs=[
            pl.BlockSpec((L, C), row),
            pl.BlockSpec((1, L, C), lambda b: (b, 0, 0)),
        ],
        out_shape=[
            jax.ShapeDtypeStruct((Bs * Ts, C), F32),
            jax.ShapeDtypeStruct((Bs, L, C), F32),
        ],
        compiler_params=_cparams(("parallel",)),
        name="gmlp_sample",
    )(z, z, ln_g.reshape(1, C), ln_b.reshape(1, C), wx, bx)


def _run_group(x, mods, P, sample):
    B, T, D = x.shape
    M = B * T
    C = P["conv_w"].shape[1]
    sbw = P["sb_w"]
    dcw = P["dc_w"]
    gc = P["gmlp_ln_g"].shape[0]
    act = F32 if sample is not None else BF16
    big = sample is None

    def gate_args(g):
        if big:
            return dict(gate=g.reshape(B, 1, D), rows_per_gate=T)
        return dict(gate=jnp.repeat(g, T, axis=0))

    outs = {}
    for layer in range(2):
        sh1, sc1, gt1, sh2, sc2, gt2 = jnp.split(mods[layer], 6, axis=-1)
        h = _norm_mod(x, P["norm_g"][layer, 0], sc1, sh1, act).reshape(M, D)
        x2 = x.reshape(M, D)
        if layer == 0:
            w_in = P["w_in_even"]
            ag = _matmul(h, w_in, n_out=2 * C)
            q = _matmul(h, w_in, col_off=2 * C, n_out=sbw, out_dtype=act)
            k = _matmul(h, w_in, col_off=2 * C + sbw, n_out=sbw)
            v = _matmul(h, w_in, col_off=2 * C + 2 * sbw, n_out=sbw)
            heads = sbw // LANES
            if big:
                a_out, conv_st = _conv_module(ag, B, T, C, None, P["conv_w"], P["conv_b"],
                                              P["conv_ln_g"], P["conv_ln_b"], BF16)
                sb = _sb_prompt(q, k, v, B, T, heads, BF16)
            else:
                a_out, conv_st = _conv_module(ag, B, T, C, sample["state_conv"], P["conv_w"],
                                              P["conv_b"], P["conv_ln_g"], P["conv_ln_b"], F32)
                sb = _sb_sample(q.reshape(B, T, sbw), k.reshape(B, T, sbw), v.reshape(B, T, sbw),
                                sample["cache_sb_k"], sample["cache_sb_v"],
                                sample["page_table"]).reshape(M, sbw)
            outs["conv_st"] = conv_st
            outs["sb_k"] = k.reshape(B, T, heads, LANES)
            outs["sb_v"] = v.reshape(B, T, heads, LANES)
            x2 = _matmul(a_out, P["w_out_even"], a2=sb, res=x2, **gate_args(gt1))
        else:
            heads = dcw // (2 * LANES)
            qg = jnp.tile(P["q_norm_g"], dcw // LANES)
            kg = jnp.tile(P["k_norm_g"], dcw // LANES)
            w_in = P["w_in_odd"]
            q = _matmul(h, w_in, n_out=dcw, epilogue="group_rmsnorm", vec=qg, out_dtype=act)
            k = _matmul(h, w_in, col_off=dcw, n_out=dcw, epilogue="group_rmsnorm", vec=kg)
            v = _matmul(h, w_in, col_off=2 * dcw, n_out=dcw)
            z = _matmul(h, w_in, col_off=3 * dcw, n_out=2 * gc, epilogue="gelu")
            if big:
                c_out = _dc_prompt(q, k, v, B, T, heads, P["rel_bias"], P["lam_vecs"],
                                   P["sub_norm_g"], BF16)
                d_out, g_v = _gmlp_prompt(z, B, T, gc, P["gmlp_ln_g"], P["gmlp_ln_b"],
                                          P["gmlp_ws"], P["gmlp_bs"], BF16)
            else:
                c_out = _dc_sample(q.reshape(B, T, dcw), k.reshape(B, T, dcw), v.reshape(B, T, dcw),
                                   sample["cache_dc_k"], sample["cache_dc_v"], sample["page_table"],
                                   P["rel_bias"], P["lam_vecs"], P["sub_norm_g"]).reshape(M, dcw)
                d_out, g_v = _gmlp_sample(z, B, T, gc, P["gmlp_ln_g"], P["gmlp_ln_b"],
                                          P["gmlp_ws"], P["gmlp_bs"])
            outs["dc_k"] = k.reshape(B, T, heads, 2, LANES)
            outs["dc_v"] = v.reshape(B, T, heads, 2 * LANES)
            outs["g_v"] = g_v
            x2 = _matmul(c_out, P["w_out_odd"], a2=d_out, res=x2, **gate_args(gt1))
        h = _norm_mod(x2.reshape(B, T, D), P["norm_g"][layer, 1], sc2, sh2, act).reshape(M, D)
        a = _matmul(h, P["w_mlp_in"], layer=layer, epilogue="relu2", out_dtype=BF16)
        x2 = _matmul(a, P["w_mlp_out"], layer=layer, res=x2, **gate_args(gt2))
        x = x2.reshape(B, T, D)
    return x, outs


def kernel(x_prompt, x_sample, state_conv, cache_sb_k, cache_sb_v, cache_dc_k, cache_dc_v, page_table, c_prompt, c_sample, norm_g, w_ada, b_ada, w_in_even, w_out_even, conv_w, conv_b, conv_ln_g, conv_ln_b, w_in_odd, w_out_odd, q_norm_g, k_norm_g, lambda_q1, lambda_k1, lambda_q2, lambda_k2, sub_norm_g, rel_bias, gmlp_ln_g, gmlp_ln_b, gmlp_ws, gmlp_bs, w_mlp_in, w_mlp_out):
    B, T, D = x_prompt.shape
    Bs, Ts, _ = x_sample.shape
    sbw = cache_sb_k.shape[2] * cache_sb_k.shape[3]
    dcw = cache_dc_v.shape[2] * cache_dc_v.shape[3]

    P = dict(
        norm_g=norm_g, conv_w=conv_w, conv_b=conv_b, conv_ln_g=conv_ln_g, conv_ln_b=conv_ln_b,
        q_norm_g=q_norm_g, k_norm_g=k_norm_g, sub_norm_g=sub_norm_g, rel_bias=rel_bias,
        gmlp_ln_g=gmlp_ln_g, gmlp_ln_b=gmlp_ln_b, gmlp_ws=gmlp_ws, gmlp_bs=gmlp_bs,
        lam_vecs=jnp.stack([lambda_q1, lambda_k1, lambda_q2, lambda_k2]).astype(F32),
        w_in_even=w_in_even.astype(BF16)[None], w_out_even=w_out_even.astype(BF16)[None],
        w_in_odd=w_in_odd.astype(BF16)[None], w_out_odd=w_out_odd.astype(BF16)[None],
        w_mlp_in=w_mlp_in.astype(BF16), w_mlp_out=w_mlp_out.astype(BF16),
        sb_w=sbw, dc_w=dcw,
    )

    nc = B + Bs
    rows = -(-nc // 16) * 16
    c_all = jnp.pad(jnp.concatenate([c_prompt, c_sample], axis=0), ((0, rows - nc), (0, 0)))
    mods = [_matmul(c_all, w_ada, layer=l, a_silu=True, epilogue="bias", vec=b_ada[l], tn_pref=512)
            for l in range(w_ada.shape[0])]
    mods_p = [m[:B] for m in mods]
    mods_s = [m[B:nc] for m in mods]

    y_p, o_p = _run_group(x_prompt, mods_p, P, None)
    sample = dict(state_conv=state_conv, page_table=page_table, cache_sb_k=cache_sb_k,
                  cache_sb_v=cache_sb_v, cache_dc_k=cache_dc_k, cache_dc_v=cache_dc_v)
    y_s, o_s = _run_group(x_sample, mods_s, P, sample)
    return (y_p, y_s, o_p["conv_st"], o_s["conv_st"],
            o_p["sb_k"], o_p["sb_v"], o_s["sb_k"], o_s["sb_v"],
            o_p["dc_k"], o_p["dc_v"], o_s["dc_k"], o_s["dc_v"],
            o_p["g_v"], o_s["g_v"])
```
